```python
import math
import jax, jax.numpy as jnp
from jax import lax
import numpy as np

D_MODEL = 2048
BATCH = 16
SEQ = 256
DEPTH = 4
DEC_BATCH = 8
DEC_SEQ = 4096
PAST_LEN = 512

GRID_W = 64
N_A_LAYERS = (DEPTH + 1) // 2
N_D_LAYERS = DEPTH // 2
MLA_HEADS = 12
MLA_NOPE = 128
MLA_ROPE = 64
MLA_V = 128
Q_LORA = 512
KV_LORA = 512
MLA_SCALE = 1.0 / math.sqrt(MLA_NOPE + MLA_ROPE)
FNET_GROUPS = 4
FNET_GC = 128
FNET_WIDTH = FNET_GROUPS * FNET_GC
IN_A = Q_LORA + KV_LORA + MLA_ROPE + FNET_WIDTH
POOL_WINDOWS = (2, 4, 8, 16)
POOL_GC = 128
POOL_WIDTH = len(POOL_WINDOWS) * POOL_GC
DIFF_HEADS = 12
DIFF_DH = 64
DIFF_QK = DIFF_HEADS * 2 * DIFF_DH
DIFF_VW = DIFF_HEADS * 2 * DIFF_DH
DIFF_SCALE = 1.0 / math.sqrt(DIFF_DH)
IN_D = POOL_WIDTH + 2 * DIFF_QK + DIFF_VW
MIX_WIDTH = 2048
D_FF = 5632
N_MOD = 9
ROPE_BASE = 10000.0
EPS = 1e-6
Q_BLOCK = 128

kernel_name = "hybrid_mla_fnet_pool_diffattn_dit_step"


def rmsnorm(x, g):
    xf = x.astype(jnp.float32)
    y = xf * lax.rsqrt(jnp.mean(xf * xf, axis=-1, keepdims=True) + EPS)
    return (y * g.astype(jnp.float32)).astype(x.dtype)


def grid_positions(rows):
    row = jnp.repeat(jnp.arange(rows), GRID_W)
    col = jnp.tile(jnp.arange(GRID_W), rows)
    return row, col


def rope_axis(x, pos):
    half = x.shape[-1] // 2
    inv = ROPE_BASE ** (-jnp.arange(half, dtype=jnp.float32) / half)
    ang = pos.astype(jnp.float32)[:, None] * inv
    ang = ang.reshape((ang.shape[0],) + (1,) * (x.ndim - 3) + (half,))
    cos, sin = jnp.cos(ang), jnp.sin(ang)
    x1, x2 = x[..., :half], x[..., half:]
    return jnp.concatenate([x1 * cos - x2 * sin, x2 * cos + x1 * sin], axis=-1).astype(x.dtype)


def rope_2d(x, row, col):
    h = x.shape[-1] // 2
    return jnp.concatenate([rope_axis(x[..., :h], row), rope_axis(x[..., h:], col)], axis=-1)


def modulate(h, shift, scale):
    return h * (1 + scale) + shift


def swiglu(h, wg, wu, wd):
    return (jax.nn.silu(h @ wg) * (h @ wu)) @ wd


def over_query_blocks(fn, qs):
    B, S = qs[0].shape[:2]
    qb = Q_BLOCK if S % Q_BLOCK == 0 else S
    nb = S // qb
    blocks = tuple(jnp.swapaxes(q.reshape((B, nb, qb) + q.shape[2:]), 0, 1) for q in qs)
    out = lax.map(lambda b: fn(*b), blocks)
    out = jnp.swapaxes(out, 0, 1)
    return out.reshape((B, S) + out.shape[3:])


def mla_block(qn, qr, kn, kr, v):
    s = (jnp.einsum('bqhd,bkhd->bhqk', qn, kn) + jnp.einsum('bqhr,bkr->bhqk', qr, kr)) * MLA_SCALE
    p = jax.nn.softmax(s.astype(jnp.float32), axis=-1).astype(v.dtype)
    return jnp.einsum('bhqk,bkhd->bqhd', p, v)


def diff_block(q, k, v, lam):
    s = jnp.einsum('bqhcd,bkhcd->bhcqk', q, k) * DIFF_SCALE
    p = jax.nn.softmax(s.astype(jnp.float32), axis=-1)
    a = (p[:, :, 0] - lam * p[:, :, 1]).astype(v.dtype)
    return jnp.einsum('bhqk,bkhe->bqhe', a, v)


def fourier_mix(fz, w_fnet):
    B, S, _ = fz.shape
    z = fz.reshape(B, S, FNET_GROUPS, FNET_GC).astype(jnp.float32)
    f = jnp.fft.fft2(z, axes=(1, 3), norm='ortho').real.astype(fz.dtype)
    return jnp.einsum('bsgc,gcd->bsgd', f, w_fnet).reshape(B, S, FNET_WIDTH)


def pool_mix(pz, w_pool, pool_scale):
    B, S, _ = pz.shape
    t = jnp.arange(S)
    outs = []
    for g, w in enumerate(POOL_WINDOWS):
        xg = pz[..., g * POOL_GC:(g + 1) * POOL_GC].astype(jnp.float32)
        c0 = jnp.pad(jnp.cumsum(xg, axis=1), ((0, 0), (1, 0), (0, 0)))
        lo = jnp.clip(t - w // 2, 0, S)
        hi = jnp.clip(t + w - w // 2, 0, S)
        mean = (c0[:, hi] - c0[:, lo]) / (hi - lo).astype(jnp.float32)[:, None]
        outs.append(mean - xg)
    pooled = jnp.stack(outs, axis=2).astype(pz.dtype)
    y = jnp.einsum('bsgc,gcd->bsgd', pooled, w_pool).reshape(B, S, POOL_WIDTH)
    return y * pool_scale


def mixer_mla_fourier(h, w_in, g_q, g_kv, w_q_up, w_kv_up, w_fnet, w_o, pos, ctx):
    B, S, _ = h.shape
    u = h @ w_in
    q_lat, ckv, krope, fz = jnp.split(u, [Q_LORA, Q_LORA + KV_LORA, Q_LORA + KV_LORA + MLA_ROPE], axis=-1)
    q = (rmsnorm(q_lat, g_q) @ w_q_up).reshape(B, S, MLA_HEADS, MLA_NOPE + MLA_ROPE)
    qn, qr = q[..., :MLA_NOPE], q[..., MLA_NOPE:]
    ckv = rmsnorm(ckv, g_kv)
    state = (ckv, krope)
    if pos is not None:
        row, col = pos
        qr = rope_2d(qr, row, col)
        ckv_all = jnp.concatenate([ckv, ctx[0]], axis=1)
        kr_all = jnp.concatenate([rope_2d(krope, row, col), ctx[1]], axis=1)
    else:
        ckv_all, kr_all = ckv, krope
    K = ckv_all.shape[1]
    kv = (ckv_all @ w_kv_up).reshape(B, K, MLA_HEADS, MLA_NOPE + MLA_V)
    kn, v = kv[..., :MLA_NOPE], kv[..., MLA_NOPE:]
    attn = over_query_blocks(lambda a, b: mla_block(a, b, kn, kr_all, v), (qn, qr))
    attn = attn.reshape(B, S, MLA_HEADS * MLA_V)
    four = fourier_mix(fz, w_fnet)
    return jnp.concatenate([attn, four], axis=-1) @ w_o, state


def mixer_pool_diff(h, w_in, lam_qk, g_sub, w_pool, pool_scale, w_o, lam_init, pos, ctx):
    B, S, _ = h.shape
    u = h @ w_in
    pz, q, k, v = jnp.split(u, [POOL_WIDTH, POOL_WIDTH + DIFF_QK, POOL_WIDTH + 2 * DIFF_QK], axis=-1)
    q = q.reshape(B, S, DIFF_HEADS, 2, DIFF_DH)
    k = k.reshape(B, S, DIFF_HEADS, 2, DIFF_DH)
    v = v.reshape(B, S, DIFF_HEADS, 2 * DIFF_DH)
    state = (k, v)
    if pos is not None:
        row, col = pos
        q = rope_2d(q, row, col)
        k_all = jnp.concatenate([rope_2d(k, row, col), ctx[0]], axis=1)
        v_all = jnp.concatenate([v, ctx[1]], axis=1)
    else:
        k_all, v_all = k, v
    lf = lam_qk.astype(jnp.float32)
    lam = jnp.exp(jnp.sum(lf[0] * lf[1])) - jnp.exp(jnp.sum(lf[2] * lf[3])) + lam_init
    o = over_query_blocks(lambda a: diff_block(a, k_all, v_all, lam), (q,))
    o = (rmsnorm(o, g_sub) * (1.0 - lam_init)).reshape(B, S, DIFF_VW)
    pool = pool_mix(pz, w_pool, pool_scale)
    return jnp.concatenate([pool, o], axis=-1) @ w_o, state


def trunk(x, cond, pos, caches, p):
    ckv_l, kr_l, dk_l, dv_l = [], [], [], []
    for l in range(DEPTH):
        mod = (jax.nn.silu(cond) @ p['w_mod'][l] + p['b_mod'][l]).reshape(cond.shape[0], 1, N_MOD, D_MODEL)
        m = [mod[:, :, i] for i in range(N_MOD)]
        h = modulate(rmsnorm(x, p['g_norm'][l, 0]), m[0], m[1])
        x = x + 0.5 * m[2] * swiglu(h, p['w_ffn_gate'][l, 0], p['w_ffn_up'][l, 0], p['w_ffn_down'][l, 0])
        h = modulate(rmsnorm(x, p['g_norm'][l, 1]), m[3], m[4])
        i = l // 2
        if l % 2 == 0:
            ctx = None if caches is None else (caches[0][:, i], caches[1][:, i])
            out, st = mixer_mla_fourier(h, p['w_in_a'][i], p['g_q'][i], p['g_kv'][i], p['w_q_up'][i],
                                        p['w_kv_up'][i], p['w_fnet'][i], p['w_o_a'][i], pos, ctx)
            ckv_l.append(st[0]); kr_l.append(st[1])
        else:
            ctx = None if caches is None else (caches[2][:, i], caches[3][:, i])
            lam_init = 0.8 - 0.6 * math.exp(-0.3 * l)
            out, st = mixer_pool_diff(h, p['w_in_d'][i], p['lam_qk'][i], p['g_sub'][i], p['w_pool'][i],
                                      p['pool_scale'][i], p['w_o_d'][i], lam_init, pos, ctx)
            dk_l.append(st[0]); dv_l.append(st[1])
        x = x + m[5] * out
        h = modulate(rmsnorm(x, p['g_norm'][l, 2]), m[6], m[7])
        x = x + 0.5 * m[8] * swiglu(h, p['w_ffn_gate'][l, 1], p['w_ffn_up'][l, 1], p['w_ffn_down'][l, 1])
    y = rmsnorm(x, p['g_final'])
    states = (jnp.stack(ckv_l, axis=1), jnp.stack(kr_l, axis=1),
              jnp.stack(dk_l, axis=1), jnp.stack(dv_l, axis=1))
    return y, states


def setup_inputs(seed: int = 0) -> dict:
    key = jax.random.key(seed)
    ks = jax.random.split(key, 32)
    f32 = jnp.float32

    def nrm(k, shape, scale=1.0):
        return jax.random.normal(k, shape, f32) * scale

    D, F = D_MODEL, D_FF
    return {
        'x_prompt': nrm(ks[0], (BATCH, SEQ, D)),
        'x_sample': nrm(ks[1], (DEC_BATCH, DEC_SEQ, D)),
        'cache_mla_ckv': nrm(ks[2], (DEC_BATCH, N_A_LAYERS, PAST_LEN, KV_LORA)),
        'cache_mla_krope': nrm(ks[3], (DEC_BATCH, N_A_LAYERS, PAST_LEN, MLA_ROPE)),
        'cache_diff_k': nrm(ks[4], (DEC_BATCH, N_D_LAYERS, PAST_LEN, DIFF_HEADS, 2, DIFF_DH)),
        'cache_diff_v': nrm(ks[5], (DEC_BATCH, N_D_LAYERS, PAST_LEN, DIFF_HEADS, 2 * DIFF_DH)),
        'c': nrm(ks[6], (DEC_BATCH, D)),
        'c_ctx': nrm(ks[7], (D,)),
        'w_mod': nrm(ks[8], (DEPTH, D, N_MOD * D), 0.5 * D ** -0.5),
        'b_mod': nrm(ks[9], (DEPTH, N_MOD * D), 0.02),
        'g_norm': 1.0 + nrm(ks[10], (DEPTH, 3, D), 0.02),
        'w_ffn_gate': nrm(ks[11], (DEPTH, 2, D, F), D ** -0.5),
        'w_ffn_up': nrm(ks[12], (DEPTH, 2, D, F), D ** -0.5),
        'w_ffn_down': nrm(ks[13], (DEPTH, 2, F, D), F ** -0.5),
        'w_in_a': nrm(ks[14], (N_A_LAYERS, D, IN_A), D ** -0.5),
        'g_q': 1.0 + nrm(ks[15], (N_A_LAYERS, Q_LORA), 0.02),
        'g_kv': 1.0 + nrm(ks[16], (N_A_LAYERS, KV_LORA), 0.02),
        'w_q_up': nrm(ks[17], (N_A_LAYERS, Q_LORA, MLA_HEADS * (MLA_NOPE + MLA_ROPE)), Q_LORA ** -0.5),
        'w_kv_up': nrm(ks[18], (N_A_LAYERS, KV_LORA, MLA_HEADS * (MLA_NOPE + MLA_V)), KV_LORA ** -0.5),
        'w_fnet': nrm(ks[19], (N_A_LAYERS, FNET_GROUPS, FNET_GC, FNET_GC), FNET_GC ** -0.5),
        'w_o_a': nrm(ks[20], (N_A_LAYERS, MIX_WIDTH, D), MIX_WIDTH ** -0.5),
        'w_in_d': nrm(ks[21], (N_D_LAYERS, D, IN_D), D ** -0.5),
        'lam_qk': nrm(ks[22], (N_D_LAYERS, 4, DIFF_DH), 0.1),
        'g_sub': 1.0 + nrm(ks[23], (N_D_LAYERS, 2 * DIFF_DH), 0.02),
        'w_pool': nrm(ks[24], (N_D_LAYERS, len(POOL_WINDOWS), POOL_GC, POOL_GC), POOL_GC ** -0.5),
        'pool_scale': 1.0 + nrm(ks[25], (N_D_LAYERS, POOL_WIDTH), 0.02),
        'w_o_d': nrm(ks[26], (N_D_LAYERS, MIX_WIDTH, D), MIX_WIDTH ** -0.5),
        'g_final': 1.0 + nrm(ks[27], (D,), 0.02),
    }


def reference(x_prompt, x_sample, cache_mla_ckv, cache_mla_krope, cache_diff_k, cache_diff_v, c, c_ctx,
              w_mod, b_mod, g_norm, w_ffn_gate, w_ffn_up, w_ffn_down,
              w_in_a, g_q, g_kv, w_q_up, w_kv_up, w_fnet, w_o_a,
              w_in_d, lam_qk, g_sub, w_pool, pool_scale, w_o_d, g_final):
    p = dict(w_mod=w_mod, b_mod=b_mod, g_norm=g_norm, w_ffn_gate=w_ffn_gate, w_ffn_up=w_ffn_up,
             w_ffn_down=w_ffn_down, w_in_a=w_in_a, g_q=g_q, g_kv=g_kv, w_q_up=w_q_up, w_kv_up=w_kv_up,
             w_fnet=w_fnet, w_o_a=w_o_a, w_in_d=w_in_d, lam_qk=lam_qk, g_sub=g_sub, w_pool=w_pool,
             pool_scale=pool_scale, w_o_d=w_o_d, g_final=g_final)
    y_prompt, st = trunk(x_prompt, c_ctx[None, :], None, None, p)
    new_mla_ckv, new_mla_krope, new_diff_k, new_diff_v = st
    rows = x_sample.shape[1] // GRID_W
    pos = grid_positions(rows)
    y_sample, _ = trunk(x_sample, c, pos, (cache_mla_ckv, cache_mla_krope, cache_diff_k, cache_diff_v), p)
    return (y_prompt, y_sample, new_mla_ckv, new_mla_krope, new_diff_k, new_diff_v)
```

```python
import functools
import math

import jax
import jax.numpy as jnp
from jax import lax
from jax.experimental import pallas as pl
from jax.experimental.pallas import tpu as pltpu

GRID_W = 64
MLA_HEADS = 12
MLA_NOPE = 128
MLA_ROPE = 64
MLA_V = 128
Q_LORA = 512
KV_LORA = 512
FNET_GROUPS = 4
FNET_GC = 128
POOL_WINDOWS = (2, 4, 8, 16)
POOL_GC = 128
DIFF_HEADS = 12
DIFF_DH = 64
N_MOD = 9
ROPE_BASE = 10000.0
EPS = 1e-6

LANES = 128
MXU_DEPTH = 256
VMEM_BYTES = 64 * 2 ** 20
VMEM_LIMIT = VMEM_BYTES - 8 * 2 ** 20

TOKEN_TILE = 512
ATTN_Q_TILE = 256
MOD_ROWS = 16

_BF = jnp.bfloat16
_F32 = jnp.float32


def _tile(n, pref):
    if n <= pref:
        return n
    t = (pref // LANES) * LANES
    while n % t:
        t -= LANES
    return t


def _params(*sem):
    return pltpu.CompilerParams(dimension_semantics=sem, vmem_limit_bytes=VMEM_LIMIT)


def _dot(a, b):
    return jnp.dot(a, b, preferred_element_type=_F32)


def _dot_t(a, b):
    return lax.dot_general(a, b, (((1,), (1,)), ((), ())), preferred_element_type=_F32)


def _rms(x, g):
    ms = jnp.mean(x * x, axis=-1, keepdims=True)
    return x * lax.rsqrt(ms + EPS) * g


def _norm_mod(x, g, shift, scale):
    return _rms(x, g) * (1.0 + scale) + shift


def _silu(x):
    return x / (1.0 + jnp.exp(-x))


def _rope(x, c, sa, sb):
    return x * c + pltpu.roll(x, LANES - 16, 1) * sa + pltpu.roll(x, 16, 1) * sb


def _mod_row(layer, tiles_per_batch):
    if tiles_per_batch is None:
        return lambda i: layer * MOD_ROWS
    return lambda i: layer * MOD_ROWS + 1 + i // tiles_per_batch


def _mod_kernel(c_ref, w_ref, b_ref, o_ref):
    s = _silu(c_ref[...]).astype(_BF)
    o_ref[...] = _dot(s, w_ref[...].astype(_BF)) + b_ref[...]


def _mod_call(cond, w_mod, b_mod):
    depth, d, n = w_mod.shape
    r = cond.shape[0]
    tn = _tile(n, 1024)
    out = pl.pallas_call(
        _mod_kernel,
        grid=(depth, n // tn),
        in_specs=[pl.BlockSpec((r, d), lambda l, j: (0, 0)),
                  pl.BlockSpec((None, d, tn), lambda l, j: (l, 0, j)),
                  pl.BlockSpec((None, 1, tn), lambda l, j: (l, 0, j))],
        out_specs=pl.BlockSpec((None, r, tn), lambda l, j: (l, 0, j)),
        out_shape=jax.ShapeDtypeStruct((depth, r, n), _F32),
        compiler_params=_params("parallel", "parallel"),
        name="mod",
    )(cond, w_mod, b_mod.reshape(depth, 1, n))
    return out.reshape(depth * r, N_MOD, d)


def _ffn_kernel(x_ref, m_ref, g_ref, wg_ref, wu_ref, wd_ref, gf_ref, o_ref, h_ref, acc_ref, *, base, final):
    j = pl.program_id(1)

    @pl.when(j == 0)
    def _():
        h = _norm_mod(x_ref[...], g_ref[...], m_ref[base:base + 1, :], m_ref[base + 1:base + 2, :])
        h_ref[...] = h.astype(_BF)
        acc_ref[...] = jnp.zeros_like(acc_ref)

    h = h_ref[...]
    a = _silu(_dot(h, wg_ref[...])) * _dot(h, wu_ref[...])
    acc_ref[...] += _dot(a.astype(_BF), wd_ref[...])

    @pl.when(j == pl.num_programs(1) - 1)
    def _():
        y = x_ref[...] + (0.5 * m_ref[base + 2:base + 3, :]) * acc_ref[...]
        if final:
            y = _rms(y, gf_ref[...])
        o_ref[...] = y


def _ffn_call(x, mod, g, wg, wu, wd, layer, sub, tm, tpb, g_final=None):
    t, d = x.shape
    f = wg.shape[-1]
    tf = _tile(f, 512)
    row = _mod_row(layer, tpb)
    final = g_final is not None
    gf = g_final if final else g
    return pl.pallas_call(
        functools.partial(_ffn_kernel, base=0 if sub == 0 else 6, final=final),
        grid=(t // tm, f // tf),
        in_specs=[pl.BlockSpec((tm, d), lambda i, j: (i, 0)),
                  pl.BlockSpec((None, N_MOD, d), lambda i, j: (row(i), 0, 0)),
                  pl.BlockSpec((1, d), lambda i, j: (0, 0)),
                  pl.BlockSpec((None, None, d, tf), lambda i, j: (layer, sub, 0, j)),
                  pl.BlockSpec((None, None, d, tf), lambda i, j: (layer, sub, 0, j)),
                  pl.BlockSpec((None, None, tf, d), lambda i, j: (layer, sub, j, 0)),
                  pl.BlockSpec((1, d), lambda i, j: (0, 0))],
        out_specs=pl.BlockSpec((tm, d), lambda i, j: (i, 0)),
        out_shape=jax.ShapeDtypeStruct((t, d), _F32),
        scratch_shapes=[pltpu.VMEM((tm, d), _BF), pltpu.VMEM((tm, d), _F32)],
        compiler_params=_params("parallel", "arbitrary"),
        name="ffn",
    )(x, mod, g, wg, wu, wd, gf)


def _oproj_kernel(a1_ref, a2_ref, w1_ref, w2_ref, x_ref, m_ref, o_ref):
    acc = _dot(a1_ref[...], w1_ref[...]) + _dot(a2_ref[...], w2_ref[...])
    o_ref[...] = x_ref[...] + m_ref[5:6, :] * acc


def _oproj_call(a1, a2, w1, w2, x, mod, layer, tm, tpb):
    t, d = x.shape
    k1, k2 = a1.shape[1], a2.shape[1]
    row = _mod_row(layer, tpb)
    return pl.pallas_call(
        _oproj_kernel,
        grid=(t // tm,),
        in_specs=[pl.BlockSpec((tm, k1), lambda i: (i, 0)),
                  pl.BlockSpec((tm, k2), lambda i: (i, 0)),
                  pl.BlockSpec((k1, d), lambda i: (0, 0)),
                  pl.BlockSpec((k2, d), lambda i: (0, 0)),
                  pl.BlockSpec((tm, d), lambda i: (i, 0)),
                  pl.BlockSpec((None, N_MOD, d), lambda i: (row(i), 0, 0))],
        out_specs=pl.BlockSpec((tm, d), lambda i: (i, 0)),
        out_shape=jax.ShapeDtypeStruct((t, d), _F32),
        compiler_params=_params("parallel"),
        name="oproj",
    )(a1, a2, w1, w2, x, mod)


def _proj_a_kernel(*refs, rope, state, heads):
    x_ref, m_ref, g_ref, win_ref, gq_ref, gkv_ref, wq_ref = refs[:7]
    refs = refs[7:]
    if rope:
        c_ref, sa_ref, sb_ref = refs[:3]
        refs = refs[3:]
    q_ref, ckv_ref, kr_ref, fz_ref = refs[:4]
    h = _norm_mod(x_ref[...], g_ref[...], m_ref[3:4, :], m_ref[4:5, :]).astype(_BF)
    u = _dot(h, win_ref[...])
    ql, kl = Q_LORA, KV_LORA
    fw = FNET_GROUPS * FNET_GC
    qn = _rms(u[:, :ql], gq_ref[...]).astype(_BF)
    ckv = _rms(u[:, ql:ql + kl], gkv_ref[...])
    kr = u[:, ql + kl + fw:]
    ckv_ref[...] = ckv.astype(_BF)
    fz_ref[...] = u[:, ql + kl:ql + kl + fw].astype(_BF)
    if state:
        refs[4][...] = ckv
        refs[5][...] = kr
    q = _dot(qn, wq_ref[...])
    hw = MLA_NOPE + LANES
    if rope:
        c, sa, sb = c_ref[...], sa_ref[...], sb_ref[...]
        kr = _rope(kr, c, sa, sb)
        for hd in range(heads):
            q_ref[:, hd * hw:hd * hw + MLA_NOPE] = q[:, hd * hw:hd * hw + MLA_NOPE].astype(_BF)
            q_ref[:, hd * hw + MLA_NOPE:(hd + 1) * hw] = _rope(q[:, hd * hw + MLA_NOPE:(hd + 1) * hw], c, sa, sb).astype(_BF)
    else:
        q_ref[...] = q.astype(_BF)
    kr_ref[...] = kr.astype(_BF)


def _proj_a_call(x, mod, g, w_in, g_q, g_kv, wq, tables, layer, tm, tpb, state):
    t, d = x.shape
    n_in = w_in.shape[1]
    nq = wq.shape[1]
    fw = FNET_GROUPS * FNET_GC
    rope = tables is not None
    row = _mod_row(layer, tpb)
    in_specs = [pl.BlockSpec((tm, d), lambda i: (i, 0)),
                pl.BlockSpec((None, N_MOD, d), lambda i: (row(i), 0, 0)),
                pl.BlockSpec((1, d), lambda i: (0, 0)),
                pl.BlockSpec((d, n_in), lambda i: (0, 0)),
                pl.BlockSpec((1, Q_LORA), lambda i: (0, 0)),
                pl.BlockSpec((1, KV_LORA), lambda i: (0, 0)),
                pl.BlockSpec((Q_LORA, nq), lambda i: (0, 0))]
    args = [x, mod, g, w_in, g_q, g_kv, wq]
    if rope:
        in_specs += [pl.BlockSpec((tm, LANES), lambda i: (i % tpb, 0))] * 3
        args += list(tables)
    shapes = [((t, nq), _BF), ((t, KV_LORA), _BF), ((t, LANES), _BF), ((t, fw), _BF)]
    if state:
        shapes += [((t, KV_LORA), _F32), ((t, LANES), _F32)]
    return pl.pallas_call(
        functools.partial(_proj_a_kernel, rope=rope, state=state, heads=MLA_HEADS),
        grid=(t // tm,),
        in_specs=in_specs,
        out_specs=[pl.BlockSpec((tm, s[1]), lambda i: (i, 0)) for s, _ in shapes],
        out_shape=[jax.ShapeDtypeStruct(s, dt) for s, dt in shapes],
        compiler_params=_params("parallel"),
        name="proj_a",
    )(*args)


def _kvup_kernel(ckv_ref, kr_ref, w_ref, k_ref, v_ref, *, heads):
    kv = _dot(ckv_ref[...], w_ref[...])
    kr = kr_ref[...]
    hw = MLA_NOPE + MLA_V
    for hd in range(heads):
        k_ref[hd, :, :MLA_NOPE] = kv[:, hd * hw:hd * hw + MLA_NOPE].astype(_BF)
        k_ref[hd, :, MLA_NOPE:] = kr
        v_ref[hd] = kv[:, hd * hw + MLA_NOPE:(hd + 1) * hw].astype(_BF)


def _kvup_call(ckv, kr, wkv, tm):
    t = ckv.shape[0]
    heads = MLA_HEADS
    return pl.pallas_call(
        functools.partial(_kvup_kernel, heads=heads),
        grid=(t // tm,),
        in_specs=[pl.BlockSpec((tm, KV_LORA), lambda i: (i, 0)),
                  pl.BlockSpec((tm, LANES), lambda i: (i, 0)),
                  pl.BlockSpec(wkv.shape, lambda i: (0, 0))],
        out_specs=[pl.BlockSpec((heads, tm, MLA_NOPE + LANES), lambda i: (0, i, 0)),
                   pl.BlockSpec((heads, tm, MLA_V), lambda i: (0, i, 0))],
        out_shape=[jax.ShapeDtypeStruct((heads, t, MLA_NOPE + LANES), _BF),
                   jax.ShapeDtypeStruct((heads, t, MLA_V), _BF)],
        compiler_params=_params("parallel"),
        name="kvup",
    )(ckv, kr, wkv)


def _mla_kernel(*refs, hb, cache):
    if cache:
        q_ref, k_ref, v_ref, kc_ref, vc_ref, o_ref = refs
    else:
        q_ref, k_ref, v_ref, o_ref = refs
    qw = MLA_NOPE + LANES
    for hd in range(hb):
        q = q_ref[:, hd * qw:(hd + 1) * qw]
        s = _dot_t(q, k_ref[hd])
        m = jnp.max(s, axis=-1, keepdims=True)
        if cache:
            sc = _dot_t(q, kc_ref[hd])
            m = jnp.maximum(m, jnp.max(sc, axis=-1, keepdims=True))
        p = jnp.exp(s - m)
        l = jnp.sum(p, axis=-1, keepdims=True)
        o = _dot(p.astype(_BF), v_ref[hd])
        if cache:
            pc = jnp.exp(sc - m)
            l = l + jnp.sum(pc, axis=-1, keepdims=True)
            o = o + _dot(pc.astype(_BF), vc_ref[hd])
        o_ref[:, hd * MLA_V:(hd + 1) * MLA_V] = (o / l).astype(_BF)


def _mla_call(q, k, v, kc, vc, batch, seq, hb):
    t = q.shape[0]
    heads = MLA_HEADS
    tq = min(ATTN_Q_TILE, seq)
    nq = seq // tq
    qw = MLA_NOPE + LANES
    cache = kc is not None
    in_specs = [pl.BlockSpec((tq, hb * qw), lambda b, h, i: (b * nq + i, h)),
                pl.BlockSpec((hb, seq, qw), lambda b, h, i: (h, b, 0)),
                pl.BlockSpec((hb, seq, MLA_V), lambda b, h, i: (h, b, 0))]
    args = [q, k, v]
    if cache:
        past = kc.shape[1] // batch
        in_specs += [pl.BlockSpec((hb, past, qw), lambda b, h, i: (h, b, 0)),
                     pl.BlockSpec((hb, past, MLA_V), lambda b, h, i: (h, b, 0))]
        args += [kc, vc]
    return pl.pallas_call(
        functools.partial(_mla_kernel, hb=hb, cache=cache),
        grid=(batch, heads // hb, nq),
        in_specs=in_specs,
        out_specs=pl.BlockSpec((tq, hb * MLA_V), lambda b, h, i: (b * nq + i, h)),
        out_shape=jax.ShapeDtypeStruct((t, heads * MLA_V), _BF),
        compiler_params=_params("parallel", "parallel", "parallel"),
        name="mla_attn",
    )(*args)


def _fourier_kernel(cs_ref, sn_ref, z_ref, cc_ref, sc_ref, w_ref, o_ref, *, norm):
    z = z_ref[...]
    a = _dot(cs_ref[...], z)
    b = _dot(sn_ref[...], z)
    gc = FNET_GC
    for g in range(FNET_GROUPS):
        f = _dot(a[:, g * gc:(g + 1) * gc].astype(_BF), cc_ref[...]) - _dot(b[:, g * gc:(g + 1) * gc].astype(_BF), sc_ref[...])
        f = (f * norm).astype(_BF)
        o_ref[:, g * gc:(g + 1) * gc] = _dot(f, w_ref[g]).astype(_BF)


def _dft_tables(n):
    k = lax.broadcasted_iota(jnp.int32, (n, n), 0) * lax.broadcasted_iota(jnp.int32, (n, n), 1) % n
    ang = k.astype(_F32) * (2.0 * math.pi / n)
    return jnp.cos(ang).astype(_BF), jnp.sin(ang).astype(_BF)


def _fourier_call(z, tabs_s, tabs_c, w, batch, seq):
    t, fw = z.shape
    tq = min(TOKEN_TILE, seq)
    nq = seq // tq
    cs, sn = tabs_s
    cc, sc = tabs_c
    gc = FNET_GC
    return pl.pallas_call(
        functools.partial(_fourier_kernel, norm=1.0 / math.sqrt(seq * gc)),
        grid=(nq, batch),
        in_specs=[pl.BlockSpec((tq, seq), lambda i, b: (i, 0)),
                  pl.BlockSpec((tq, seq), lambda i, b: (i, 0)),
                  pl.BlockSpec((seq, fw), lambda i, b: (b, 0)),
                  pl.BlockSpec((gc, gc), lambda i, b: (0, 0)),
                  pl.BlockSpec((gc, gc), lambda i, b: (0, 0)),
                  pl.BlockSpec(w.shape, lambda i, b: (0, 0, 0))],
        out_specs=pl.BlockSpec((tq, fw), lambda i, b: (b * nq + i, 0)),
        out_shape=jax.ShapeDtypeStruct((t, fw), _BF),
        compiler_params=_params("parallel", "parallel"),
        name="fourier",
    )(cs, sn, z, cc, sc, w)


def _proj_d_kernel(*refs, rope, state, npz, nqk, chunks):
    x_ref, m_ref, g_ref, w_ref = refs[:4]
    refs = refs[4:]
    if rope:
        c_ref, sa_ref, sb_ref = refs[:3]
        refs = refs[3:]
    pz_ref, qkv_ref = refs[:2]
    h_ref = refs[-1]
    j = pl.program_id(1)

    @pl.when(j == 0)
    def _():
        h_ref[...] = _norm_mod(x_ref[...], g_ref[...], m_ref[3:4, :], m_ref[4:5, :]).astype(_BF)

    acc = _dot(h_ref[...], w_ref[...])

    @pl.when(j < npz)
    def _():
        pz_ref[...] = acc

    if rope:
        @pl.when((j >= npz) & (j < npz + 2 * nqk))
        def _():
            c, sa, sb = c_ref[...], sa_ref[...], sb_ref[...]
            for k in range(chunks):
                qkv_ref[:, k * LANES:(k + 1) * LANES] = _rope(acc[:, k * LANES:(k + 1) * LANES], c, sa, sb).astype(_BF)

        @pl.when(j >= npz + 2 * nqk)
        def _():
            qkv_ref[...] = acc.astype(_BF)
    else:
        @pl.when(j >= npz)
        def _():
            qkv_ref[...] = acc.astype(_BF)

    if state:
        @pl.when(j >= npz + nqk)
        def _():
            refs[2][...] = acc


def _proj_d_call(x, mod, g, w_in, tables, layer, tm, tpb, state):
    t, d = x.shape
    n = w_in.shape[1]
    pw = len(POOL_WINDOWS) * POOL_GC
    qk = DIFF_HEADS * 2 * DIFF_DH
    tn = math.gcd(math.gcd(pw, qk), 512)
    npz, nqk = pw // tn, qk // tn
    rope = tables is not None
    row = _mod_row(layer, tpb)
    in_specs = [pl.BlockSpec((tm, d), lambda i, j: (i, 0)),
                pl.BlockSpec((None, N_MOD, d), lambda i, j: (row(i), 0, 0)),
                pl.BlockSpec((1, d), lambda i, j: (0, 0)),
                pl.BlockSpec((d, tn), lambda i, j: (0, j))]
    args = [x, mod, g, w_in]
    if rope:
        in_specs += [pl.BlockSpec((tm, LANES), lambda i, j: (i % tpb, 0))] * 3
        args += list(tables)
    out_specs = [pl.BlockSpec((tm, tn), lambda i, j: (i, jnp.minimum(j, npz - 1))),
                 pl.BlockSpec((tm, tn), lambda i, j: (i, jnp.maximum(j - npz, 0)))]
    out_shape = [jax.ShapeDtypeStruct((t, pw), _F32), jax.ShapeDtypeStruct((t, 3 * qk), _BF)]
    if state:
        out_specs.append(pl.BlockSpec((tm, tn), lambda i, j: (i, jnp.maximum(j - npz - nqk, 0))))
        out_shape.append(jax.ShapeDtypeStruct((t, 2 * qk), _F32))
    return pl.pallas_call(
        functools.partial(_proj_d_kernel, rope=rope, state=state, npz=npz, nqk=nqk, chunks=tn // LANES),
        grid=(t // tm, n // tn),
        in_specs=in_specs,
        out_specs=out_specs,
        out_shape=out_shape,
        scratch_shapes=[pltpu.VMEM((tm, d), _BF)],
        compiler_params=_params("parallel", "arbitrary"),
        name="proj_d",
    )(*args)


def _diff_kernel(*refs, cache, lam_init):
    if cache:
        q_ref, k_ref, v_ref, kc_ref, vc_ref, lam_ref, g_ref, o_ref = refs
    else:
        q_ref, k_ref, v_ref, lam_ref, g_ref, o_ref = refs
    lq = lam_ref[...]
    lam = (jnp.exp(jnp.sum(lq[0:1] * lq[1:2], axis=-1, keepdims=True))
           - jnp.exp(jnp.sum(lq[2:3] * lq[3:4], axis=-1, keepdims=True)) + lam_init)
    q = q_ref[...]
    lane = lax.broadcasted_iota(jnp.int32, q.shape, 1)
    zero = jnp.zeros_like(q)
    halves = (jnp.where(lane < DIFF_DH, q, zero), jnp.where(lane >= DIFF_DH, q, zero))
    k = k_ref[...]
    parts = []
    for qc in halves:
        s = _dot_t(qc, k)
        m = jnp.max(s, axis=-1, keepdims=True)
        if cache:
            sc = _dot_t(qc, kc_ref[...])
            m = jnp.maximum(m, jnp.max(sc, axis=-1, keepdims=True))
        e = jnp.exp(s - m)
        l = jnp.sum(e, axis=-1, keepdims=True)
        ec = None
        if cache:
            ec = jnp.exp(sc - m)
            l = l + jnp.sum(ec, axis=-1, keepdims=True)
        parts.append((e, ec, l))
    (e0, ec0, l0), (e1, ec1, l1) = parts
    r0 = 1.0 / l0
    r1 = lam / l1
    o = _dot((e0 * r0 - e1 * r1).astype(_BF), v_ref[...])
    if cache:
        o = o + _dot((ec0 * r0 - ec1 * r1).astype(_BF), vc_ref[...])
    o_ref[...] = (_rms(o, g_ref[...]) * (1.0 - lam_init)).astype(_BF)


def _diff_call(qkv, kc, vc, lam_qk, g_sub, batch, seq, lam_init):
    t = qkv.shape[0]
    heads = DIFF_HEADS
    hw = 2 * DIFF_DH
    tq = min(ATTN_Q_TILE, seq)
    nq = seq // tq
    cache = kc is not None
    in_specs = [pl.BlockSpec((tq, hw), lambda b, h, i: (b * nq + i, h)),
                pl.BlockSpec((seq, hw), lambda b, h, i: (b, heads + h)),
                pl.BlockSpec((seq, hw), lambda b, h, i: (b, 2 * heads + h))]
    args = [qkv, qkv, qkv]
    if cache:
        past = kc.shape[0] // batch
        in_specs += [pl.BlockSpec((past, hw), lambda b, h, i: (b, h))] * 2
        args += [kc, vc]
    in_specs += [pl.BlockSpec(lam_qk.shape, lambda b, h, i: (0, 0)),
                 pl.BlockSpec((1, hw), lambda b, h, i: (0, 0))]
    args += [lam_qk, g_sub]
    return pl.pallas_call(
        functools.partial(_diff_kernel, cache=cache, lam_init=lam_init),
        grid=(batch, heads, nq),
        in_specs=in_specs,
        out_specs=pl.BlockSpec((tq, hw), lambda b, h, i: (b * nq + i, h)),
        out_shape=jax.ShapeDtypeStruct((t, heads * hw), _BF),
        compiler_params=_params("parallel", "parallel", "parallel"),
        name="diff_attn",
    )(*args)


POOL_PAD = 16


def _pool_kernel(pz_ref, w_ref, sc_ref, o_ref, pad_ref, *, seq):
    gc = POOL_GC
    t = lax.broadcasted_iota(jnp.int32, (seq, gc), 0)
    zeros = jnp.zeros((POOL_PAD, gc), _F32)
    pad_ref[0:POOL_PAD, :] = zeros
    pad_ref[POOL_PAD + seq:2 * POOL_PAD + seq, :] = zeros
    for g, w in enumerate(POOL_WINDOWS):
        x = pz_ref[:, g * gc:(g + 1) * gc]
        pad_ref[POOL_PAD:POOL_PAD + seq, :] = x
        lo, hi = w // 2, w - w // 2
        tot = pad_ref[POOL_PAD - lo:POOL_PAD - lo + seq, :]
        for dlt in range(-lo + 1, hi):
            tot = tot + pad_ref[POOL_PAD + dlt:POOL_PAD + dlt + seq, :]
        cnt = (jnp.minimum(t + hi, seq) - jnp.maximum(t - lo, 0)).astype(_F32)
        pooled = (tot / cnt - x).astype(_BF)
        y = _dot(pooled, w_ref[g]) * sc_ref[:, g * gc:(g + 1) * gc]
        o_ref[:, g * gc:(g + 1) * gc] = y.astype(_BF)


def _pool_call(pz, w_pool, pool_scale, batch, seq):
    t, pw = pz.shape
    assert max(POOL_WINDOWS) <= POOL_PAD
    return pl.pallas_call(
        functools.partial(_pool_kernel, seq=seq),
        grid=(batch,),
        in_specs=[pl.BlockSpec((seq, pw), lambda b: (b, 0)),
                  pl.BlockSpec(w_pool.shape, lambda b: (0, 0, 0)),
                  pl.BlockSpec((1, pw), lambda b: (0, 0))],
        out_specs=pl.BlockSpec((seq, pw), lambda b: (b, 0)),
        out_shape=jax.ShapeDtypeStruct((t, pw), _BF),
        scratch_shapes=[pltpu.VMEM((seq + 2 * POOL_PAD, POOL_GC), _F32)],
        compiler_params=_params("parallel"),
        name="pool",
    )(pz, w_pool, pool_scale)


def _rope_tables(seq, reps, pad):
    pos = jnp.arange(seq)
    half = MLA_ROPE // 4
    inv = ROPE_BASE ** (-jnp.arange(half, dtype=_F32) / half)
    ang_r = (pos // GRID_W).astype(_F32)[:, None] * inv
    ang_c = (pos % GRID_W).astype(_F32)[:, None] * inv
    cos = jnp.concatenate([jnp.cos(ang_r)] * 2 + [jnp.cos(ang_c)] * 2, axis=-1)
    sin_r, sin_c = jnp.sin(ang_r), jnp.sin(ang_c)
    z = jnp.zeros_like(sin_r)
    sa = jnp.concatenate([-sin_r, z, -sin_c, z], axis=-1)
    sb = jnp.concatenate([z, sin_r, z, sin_c], axis=-1)
    cos, sa, sb = (jnp.tile(a, (1, reps)) for a in (cos, sa, sb))
    if pad:
        cos = jnp.concatenate([cos, jnp.ones((seq, pad), _F32)], axis=-1)
        sa = jnp.concatenate([sa, jnp.zeros((seq, pad), _F32)], axis=-1)
        sb = jnp.concatenate([sb, jnp.zeros((seq, pad), _F32)], axis=-1)
    return cos, sa, sb


def _prep_weights(w_in_a, w_q_up, w_kv_up, w_in_d):
    na, d, _ = w_in_a.shape
    ql, kl, fw = Q_LORA, KV_LORA, FNET_GROUPS * FNET_GC
    q, ckv, kr, fz = (w_in_a[..., :ql], w_in_a[..., ql:ql + kl], w_in_a[..., ql + kl:ql + kl + MLA_ROPE],
                      w_in_a[..., ql + kl + MLA_ROPE:])
    w_in = jnp.concatenate([q, ckv, fz, kr, jnp.zeros((na, d, LANES - MLA_ROPE), _F32)], axis=-1).astype(_BF)
    scale = 1.0 / math.sqrt(MLA_NOPE + MLA_ROPE)
    wq = (w_q_up * scale).reshape(na, ql, MLA_HEADS, MLA_NOPE + MLA_ROPE)
    wq = jnp.pad(wq, ((0, 0), (0, 0), (0, 0), (0, LANES - MLA_ROPE))).reshape(na, ql, -1).astype(_BF)
    wkv = w_kv_up.astype(_BF)
    pw, qk = len(POOL_WINDOWS) * POOL_GC, DIFF_HEADS * 2 * DIFF_DH
    wd = jnp.concatenate([w_in_d[..., :pw], w_in_d[..., pw:pw + qk] * (1.0 / math.sqrt(DIFF_DH)),
                          w_in_d[..., pw + qk:]], axis=-1).astype(_BF)
    return w_in, wq, wkv, wd


def _trunk(x, mod, batch, seq, caches, p, tpb_of):
    t, d = x.shape
    depth = p["g_norm"].shape[0]
    decode = caches is not None
    tm = min(TOKEN_TILE, seq) if decode else min(TOKEN_TILE, t)
    tpb = seq // tm if decode else None
    states = ([], [], [], [])
    if decode:
        tab_mla = _rope_tables(seq, 1, LANES - MLA_ROPE)
        tab_diff = _rope_tables(seq, LANES // DIFF_DH, 0)
    else:
        tab_mla = tab_diff = None
    dft_s = _dft_tables(seq)
    dft_c = _dft_tables(FNET_GC)
    aw = MLA_HEADS * MLA_V
    pw = len(POOL_WINDOWS) * POOL_GC
    qk = DIFF_HEADS * 2 * DIFF_DH
    for l in range(depth):
        i = l // 2
        g = p["g_norm"][l]
        x = _ffn_call(x, mod, g[0:1], p["wg"], p["wu"], p["wd"], l, 0, tm, tpb)
        if l % 2 == 0:
            outs = _proj_a_call(x, mod, g[1:2], p["w_in_a"][i], p["g_q"][i:i + 1], p["g_kv"][i:i + 1], p["wq"][i],
                                tab_mla, l, tm, tpb, not decode)
            q, ckv, kr, fz = outs[:4]
            if not decode:
                states[0].append(outs[4])
                states[1].append(outs[5][:, :MLA_ROPE])
            k, v = _kvup_call(ckv, kr, p["wkv"][i], tm)
            kc = vc = None
            if decode:
                past = caches[0].shape[2]
                c_ckv = caches[0][:, i].reshape(batch * past, KV_LORA).astype(_BF)
                c_kr = jnp.pad(caches[1][:, i].reshape(batch * past, MLA_ROPE), ((0, 0), (0, LANES - MLA_ROPE))).astype(_BF)
                kc, vc = _kvup_call(c_ckv, c_kr, p["wkv"][i], min(TOKEN_TILE, past))
            attn = _mla_call(q, k, v, kc, vc, batch, seq, 1 if decode else MLA_HEADS)
            four = _fourier_call(fz, dft_s, dft_c, p["w_fnet"][i], batch, seq)
            x = _oproj_call(attn, four, p["w_o_a"][i, :aw], p["w_o_a"][i, aw:], x, mod, l, tm, tpb)
        else:
            lam_init = 0.8 - 0.6 * math.exp(-0.3 * l)
            outs = _proj_d_call(x, mod, g[1:2], p["w_in_d"][i], tab_diff, l, tm, tpb, not decode)
            pz, qkv = outs[:2]
            if not decode:
                states[2].append(outs[2][:, :qk])
                states[3].append(outs[2][:, qk:])
            kc = vc = None
            if decode:
                past = caches[2].shape[2]
                kc = caches[2][:, i].reshape(batch * past, qk).astype(_BF)
                vc = caches[3][:, i].reshape(batch * past, qk).astype(_BF)
            o = _diff_call(qkv, kc, vc, p["lam_qk"][i], p["g_sub"][i:i + 1], batch, seq, lam_init)
            pool = _pool_call(pz, p["w_pool"][i], p["pool_scale"][i:i + 1], batch, seq)
            x = _oproj_call(pool, o, p["w_o_d"][i, :pw], p["w_o_d"][i, pw:], x, mod, l, tm, tpb)
        x = _ffn_call(x, mod, g[2:3], p["wg"], p["wu"], p["wd"], l, 1, tm, tpb,
                      g_final=p["g_final"] if l == depth - 1 else None)
    return x, states


def kernel(x_prompt, x_sample, cache_mla_ckv, cache_mla_krope, cache_diff_k, cache_diff_v, c, c_ctx, w_mod, b_mod, g_norm, w_ffn_gate, w_ffn_up, w_ffn_down, w_in_a, g_q, g_kv, w_q_up, w_kv_up, w_fnet, w_o_a, w_in_d, lam_qk, g_sub, w_pool, pool_scale, w_o_d, g_final):
    batch, seq, d = x_prompt.shape
    dbatch, dseq, _ = x_sample.shape
    assert 1 + dbatch <= MOD_ROWS
    cond = jnp.concatenate([c_ctx[None, :], c, jnp.zeros((MOD_ROWS - 1 - dbatch, d), _F32)], axis=0)
    mod = _mod_call(cond, w_mod, b_mod)

    w_in, wq, wkv, wd_in = _prep_weights(w_in_a, w_q_up, w_kv_up, w_in_d)
    p = dict(g_norm=g_norm, wg=w_ffn_gate.astype(_BF), wu=w_ffn_up.astype(_BF), wd=w_ffn_down.astype(_BF),
             w_in_a=w_in, g_q=g_q, g_kv=g_kv, wq=wq, wkv=wkv, w_fnet=w_fnet.astype(_BF), w_o_a=w_o_a.astype(_BF),
             w_in_d=wd_in, lam_qk=lam_qk, g_sub=g_sub, w_pool=w_pool.astype(_BF), pool_scale=pool_scale,
             w_o_d=w_o_d.astype(_BF), g_final=g_final[None, :])

    y_p, st = _trunk(x_prompt.reshape(batch * seq, d), mod, batch, seq, None, p, None)
    caches = (cache_mla_ckv, cache_mla_krope, cache_diff_k, cache_diff_v)
    y_s, _ = _trunk(x_sample.reshape(dbatch * dseq, d), mod, dbatch, dseq, caches, p, None)

    new_ckv = jnp.stack([s.reshape(batch, seq, KV_LORA) for s in st[0]], axis=1)
    new_kr = jnp.stack([s.reshape(batch, seq, MLA_ROPE) for s in st[1]], axis=1)
    new_k = jnp.stack([s.reshape(batch, seq, DIFF_HEADS, 2, DIFF_DH) for s in st[2]], axis=1)
    new_v = jnp.stack([s.reshape(batch, seq, DIFF_HEADS, 2 * DIFF_DH) for s in st[3]], axis=1)
    return (y_p.reshape(batch, seq, d), y_s.reshape(dbatch, dseq, d), new_ckv, new_kr, new_k, new_v)
```

```python
import functools
import math

import jax
import jax.numpy as jnp
from jax import lax
from jax.experimental import pallas as pl
from jax.experimental.pallas import tpu as pltpu

GRID_W = 64
MLA_HEADS = 12
MLA_NOPE = 128
MLA_ROPE = 64
MLA_V = 128
Q_LORA = 512
KV_LORA = 512
FNET_GROUPS = 4
FNET_GC = 128
POOL_WINDOWS = (2, 4, 8, 16)
POOL_GC = 128
DIFF_HEADS = 12
DIFF_DH = 64
N_MOD = 9
ROPE_BASE = 10000.0
EPS = 1e-6
LOG2_E = math.log2(math.e)

LANES = 128
MXU_DEPTH = 256
VMEM_BYTES = 64 * 2 ** 20
VMEM_LIMIT = VMEM_BYTES - 8 * 2 ** 20

TOKEN_TILE = 512
PROJ_D_TILE = 1024
MLA_Q_TILE = 256
MLA_UNITS = 4
DIFF_Q_TILE = 256
DIFF_UNITS = 4
ATTN_KEY_CHUNK = 512
MOD_ROWS = 16

_BF = jnp.bfloat16
_F32 = jnp.float32


def _tile(n, pref):
    if n <= pref:
        return n
    t = (pref // LANES) * LANES
    while n % t:
        t -= LANES
    return t


def _params(*sem):
    return pltpu.CompilerParams(dimension_semantics=sem, vmem_limit_bytes=VMEM_LIMIT)


def _dot(a, b):
    return jnp.dot(a, b, preferred_element_type=_F32)


def _dot_t(a, b):
    return lax.dot_general(a, b, (((1,), (1,)), ((), ())), preferred_element_type=_F32)


def _rms(x, g):
    ms = jnp.mean(x * x, axis=-1, keepdims=True)
    return x * lax.rsqrt(ms + EPS) * g


def _norm_mod(x, g, shift, scale):
    return _rms(x, g) * (1.0 + scale) + shift


def _silu(x):
    return x / (1.0 + jnp.exp(-x))


def _rope(x, c, sa, sb):
    return x * c + pltpu.roll(x, LANES - 16, 1) * sa + pltpu.roll(x, 16, 1) * sb


def _mod_row(layer, tiles_per_batch):
    if tiles_per_batch is None:
        return lambda i: layer * MOD_ROWS
    return lambda i: layer * MOD_ROWS + 1 + i // tiles_per_batch


def _mod_kernel(c_ref, w_ref, b_ref, o_ref):
    s = _silu(c_ref[...]).astype(_BF)
    o_ref[...] = _dot(s, w_ref[...].astype(_BF)) + b_ref[...]


def _mod_call(cond, w_mod, b_mod):
    depth, d, n = w_mod.shape
    r = cond.shape[0]
    tn = _tile(n, 1024)
    out = pl.pallas_call(
        _mod_kernel,
        grid=(depth, n // tn),
        in_specs=[pl.BlockSpec((r, d), lambda l, j: (0, 0)),
                  pl.BlockSpec((None, d, tn), lambda l, j: (l, 0, j)),
                  pl.BlockSpec((None, 1, tn), lambda l, j: (l, 0, j))],
        out_specs=pl.BlockSpec((None, r, tn), lambda l, j: (l, 0, j)),
        out_shape=jax.ShapeDtypeStruct((depth, r, n), _F32),
        compiler_params=_params("parallel", "parallel"),
        name="mod",
    )(cond, w_mod, b_mod.reshape(depth, 1, n))
    return out.reshape(depth * r, N_MOD, d)


def _ffn_kernel(x_ref, m_ref, g_ref, wg_ref, wu_ref, wd_ref, gf_ref, o_ref, h_ref, acc_ref, *, base, final):
    j = pl.program_id(1)

    @pl.when(j == 0)
    def _():
        h = _norm_mod(x_ref[...], g_ref[...], m_ref[base:base + 1, :], m_ref[base + 1:base + 2, :])
        h_ref[...] = h.astype(_BF)
        acc_ref[...] = jnp.zeros_like(acc_ref)

    h = h_ref[...]
    a = _silu(_dot(h, wg_ref[...])) * _dot(h, wu_ref[...])
    acc_ref[...] += _dot(a.astype(_BF), wd_ref[...])

    @pl.when(j == pl.num_programs(1) - 1)
    def _():
        y = x_ref[...] + (0.5 * m_ref[base + 2:base + 3, :]) * acc_ref[...]
        if final:
            y = _rms(y, gf_ref[...])
        o_ref[...] = y


def _ffn_call(x, mod, g, wg, wu, wd, layer, sub, tm, tpb, g_final=None):
    t, d = x.shape
    f = wg.shape[-1]
    tf = _tile(f, 512)
    row = _mod_row(layer, tpb)
    final = g_final is not None
    gf = g_final if final else g
    return pl.pallas_call(
        functools.partial(_ffn_kernel, base=0 if sub == 0 else 6, final=final),
        grid=(t // tm, f // tf),
        in_specs=[pl.BlockSpec((tm, d), lambda i, j: (i, 0)),
                  pl.BlockSpec((None, N_MOD, d), lambda i, j: (row(i), 0, 0)),
                  pl.BlockSpec((1, d), lambda i, j: (0, 0)),
                  pl.BlockSpec((None, None, d, tf), lambda i, j: (layer, sub, 0, j)),
                  pl.BlockSpec((None, None, d, tf), lambda i, j: (layer, sub, 0, j)),
                  pl.BlockSpec((None, None, tf, d), lambda i, j: (layer, sub, j, 0)),
                  pl.BlockSpec((1, d), lambda i, j: (0, 0))],
        out_specs=pl.BlockSpec((tm, d), lambda i, j: (i, 0)),
        out_shape=jax.ShapeDtypeStruct((t, d), _F32),
        scratch_shapes=[pltpu.VMEM((tm, d), _BF), pltpu.VMEM((tm, d), _F32)],
        compiler_params=_params("parallel", "arbitrary"),
        name="ffn",
    )(x, mod, g, wg, wu, wd, gf)


def _oproj_kernel(a1_ref, a2_ref, w1_ref, w2_ref, x_ref, m_ref, o_ref):
    acc = _dot(a1_ref[...], w1_ref[...]) + _dot(a2_ref[...], w2_ref[...])
    o_ref[...] = x_ref[...] + m_ref[5:6, :] * acc


def _oproj_call(a1, a2, w1, w2, x, mod, layer, tm, tpb):
    t, d = x.shape
    k1, k2 = a1.shape[1], a2.shape[1]
    row = _mod_row(layer, tpb)
    return pl.pallas_call(
        _oproj_kernel,
        grid=(t // tm,),
        in_specs=[pl.BlockSpec((tm, k1), lambda i: (i, 0)),
                  pl.BlockSpec((tm, k2), lambda i: (i, 0)),
                  pl.BlockSpec((k1, d), lambda i: (0, 0)),
                  pl.BlockSpec((k2, d), lambda i: (0, 0)),
                  pl.BlockSpec((tm, d), lambda i: (i, 0)),
                  pl.BlockSpec((None, N_MOD, d), lambda i: (row(i), 0, 0))],
        out_specs=pl.BlockSpec((tm, d), lambda i: (i, 0)),
        out_shape=jax.ShapeDtypeStruct((t, d), _F32),
        compiler_params=_params("parallel"),
        name="oproj",
    )(a1, a2, w1, w2, x, mod)


def _proj_a_kernel(*refs, rope, state, heads):
    x_ref, m_ref, g_ref, win_ref, gq_ref, gkv_ref, wq_ref = refs[:7]
    refs = refs[7:]
    if rope:
        c_ref, sa_ref, sb_ref = refs[:3]
        refs = refs[3:]
    q_ref, ckv_ref, kr_ref, fz_ref = refs[:4]
    h = _norm_mod(x_ref[...], g_ref[...], m_ref[3:4, :], m_ref[4:5, :]).astype(_BF)
    u = _dot(h, win_ref[...])
    ql, kl = Q_LORA, KV_LORA
    fw = FNET_GROUPS * FNET_GC
    qn = _rms(u[:, :ql], gq_ref[...]).astype(_BF)
    ckv = _rms(u[:, ql:ql + kl], gkv_ref[...])
    kr = u[:, ql + kl + fw:]
    ckv_ref[...] = ckv.astype(_BF)
    fz_ref[...] = u[:, ql + kl:ql + kl + fw].astype(_BF)
    if state:
        refs[4][...] = ckv
        refs[5][...] = kr
    q = _dot(qn, wq_ref[...])
    hw = MLA_NOPE + LANES
    if rope:
        c, sa, sb = c_ref[...], sa_ref[...], sb_ref[...]
        kr = _rope(kr, c, sa, sb)
        for hd in range(heads):
            q_ref[:, hd * hw:hd * hw + MLA_NOPE] = q[:, hd * hw:hd * hw + MLA_NOPE].astype(_BF)
            q_ref[:, hd * hw + MLA_NOPE:(hd + 1) * hw] = _rope(q[:, hd * hw + MLA_NOPE:(hd + 1) * hw], c, sa, sb).astype(_BF)
    else:
        q_ref[...] = q.astype(_BF)
    kr_ref[...] = kr.astype(_BF)


def _proj_a_call(x, mod, g, w_in, g_q, g_kv, wq, tables, layer, tm, tpb, state):
    t, d = x.shape
    n_in = w_in.shape[1]
    nq = wq.shape[1]
    fw = FNET_GROUPS * FNET_GC
    rope = tables is not None
    row = _mod_row(layer, tpb)
    in_specs = [pl.BlockSpec((tm, d), lambda i: (i, 0)),
                pl.BlockSpec((None, N_MOD, d), lambda i: (row(i), 0, 0)),
                pl.BlockSpec((1, d), lambda i: (0, 0)),
                pl.BlockSpec((d, n_in), lambda i: (0, 0)),
                pl.BlockSpec((1, Q_LORA), lambda i: (0, 0)),
                pl.BlockSpec((1, KV_LORA), lambda i: (0, 0)),
                pl.BlockSpec((Q_LORA, nq), lambda i: (0, 0))]
    args = [x, mod, g, w_in, g_q, g_kv, wq]
    if rope:
        in_specs += [pl.BlockSpec((tm, LANES), lambda i: (i % tpb, 0))] * 3
        args += list(tables)
    shapes = [((t, nq), _BF), ((t, KV_LORA), _BF), ((t, LANES), _BF), ((t, fw), _BF)]
    if state:
        shapes += [((t, KV_LORA), _F32), ((t, LANES), _F32)]
    return pl.pallas_call(
        functools.partial(_proj_a_kernel, rope=rope, state=state, heads=MLA_HEADS),
        grid=(t // tm,),
        in_specs=in_specs,
        out_specs=[pl.BlockSpec((tm, s[1]), lambda i: (i, 0)) for s, _ in shapes],
        out_shape=[jax.ShapeDtypeStruct(s, dt) for s, dt in shapes],
        compiler_params=_params("parallel"),
        name="proj_a",
    )(*args)


def _kvup_kernel(ckv_ref, kr_ref, w_ref, k_ref, v_ref, *, heads):
    kv = _dot(ckv_ref[...], w_ref[...])
    kr = kr_ref[...]
    hw = MLA_NOPE + MLA_V
    for hd in range(heads):
        k_ref[hd, :, :MLA_NOPE] = kv[:, hd * hw:hd * hw + MLA_NOPE].astype(_BF)
        k_ref[hd, :, MLA_NOPE:] = kr
        v_ref[hd] = kv[:, hd * hw + MLA_NOPE:(hd + 1) * hw].T.astype(_BF)


def _kvup_call(ckv, kr, wkv, tm):
    t = ckv.shape[0]
    heads = MLA_HEADS
    return pl.pallas_call(
        functools.partial(_kvup_kernel, heads=heads),
        grid=(t // tm,),
        in_specs=[pl.BlockSpec((tm, KV_LORA), lambda i: (i, 0)),
                  pl.BlockSpec((tm, LANES), lambda i: (i, 0)),
                  pl.BlockSpec(wkv.shape, lambda i: (0, 0))],
        out_specs=[pl.BlockSpec((heads, tm, MLA_NOPE + LANES), lambda i: (0, i, 0)),
                   pl.BlockSpec((heads, MLA_V, tm), lambda i: (0, 0, i))],
        out_shape=[jax.ShapeDtypeStruct((heads, t, MLA_NOPE + LANES), _BF),
                   jax.ShapeDtypeStruct((heads, MLA_V, t), _BF)],
        compiler_params=_params("parallel"),
        name="kvup",
    )(ckv, kr, wkv)


def _interleave(*stages):
    live = list(stages)
    while live:
        for st in list(live):
            try:
                next(st)
            except StopIteration:
                live.remove(st)


def _key_chunks(seq, past):
    kc = min(ATTN_KEY_CHUNK, seq)
    chunks = [(False, c * kc, kc, c * kc) for c in range(seq // kc)]
    if past:
        pc = min(ATTN_KEY_CHUNK, past)
        chunks += [(True, c * pc, pc, seq + c * pc) for c in range(past // pc)]
    return chunks


def _attn_units(seq, heads, q_tile, n_units):
    tq = min(q_tile, seq)
    nqt = min(n_units, seq // tq)
    hb = math.gcd(heads, max(n_units // nqt, 1))
    return hb, nqt, tq


def _mla_kernel(*refs, hb, nqt, tq, seq, past):
    if past:
        q_ref, k_ref, v_ref, kc_ref, vc_ref, o_ref, *s_refs = refs
    else:
        q_ref, k_ref, v_ref, o_ref, *s_refs = refs
        kc_ref = vc_ref = None
    qw = MLA_NOPE + LANES
    chunks = _key_chunks(seq, past)
    units = [(hd, qt) for hd in range(hb) for qt in range(nqt)]
    col_max = [None] * len(units)

    def scores(u):
        hd, qt = units[u]
        q = q_ref[qt * tq:(qt + 1) * tq, hd * qw:(hd + 1) * qw]
        m = None
        for is_cache, off, n, row in chunks:
            kr = kc_ref if is_cache else k_ref
            s = _dot_t(kr[hd, off:off + n, :], q)
            s_refs[u % 2][row:row + n, :] = s
            mc = jnp.max(s, axis=0, keepdims=True)
            m = mc if m is None else jnp.maximum(m, mc)
            yield
        col_max[u] = m

    def softmax_pv(u):
        hd, qt = units[u]
        m = col_max[u]
        l = acc = None
        for is_cache, off, n, row in chunks:
            vr = vc_ref if is_cache else v_ref
            p = jnp.exp2(s_refs[u % 2][row:row + n, :] - m)
            lc = jnp.sum(p, axis=0, keepdims=True)
            oc = _dot(vr[hd, :, off:off + n], p.astype(_BF))
            l = lc if l is None else l + lc
            acc = oc if acc is None else acc + oc
            yield
        o_ref[qt * tq:(qt + 1) * tq, hd * MLA_V:(hd + 1) * MLA_V] = (acc / l).T.astype(_BF)

    _interleave(scores(0))
    for u in range(len(units)):
        if u + 1 < len(units):
            _interleave(scores(u + 1), softmax_pv(u))
        else:
            _interleave(softmax_pv(u))


def _mla_call(q, k, v, kc, vc, batch, seq):
    t = q.shape[0]
    heads = MLA_HEADS
    hb, nqt, tq = _attn_units(seq, heads, MLA_Q_TILE, MLA_UNITS)
    rows = nqt * tq
    nq = seq // rows
    qw = MLA_NOPE + LANES
    past = 0 if kc is None else kc.shape[1] // batch
    in_specs = [pl.BlockSpec((rows, hb * qw), lambda b, h, i: (b * nq + i, h)),
                pl.BlockSpec((hb, seq, qw), lambda b, h, i: (h, b, 0)),
                pl.BlockSpec((hb, MLA_V, seq), lambda b, h, i: (h, 0, b))]
    args = [q, k, v]
    if past:
        in_specs += [pl.BlockSpec((hb, past, qw), lambda b, h, i: (h, b, 0)),
                     pl.BlockSpec((hb, MLA_V, past), lambda b, h, i: (h, 0, b))]
        args += [kc, vc]
    return pl.pallas_call(
        functools.partial(_mla_kernel, hb=hb, nqt=nqt, tq=tq, seq=seq, past=past),
        grid=(batch, heads // hb, nq),
        in_specs=in_specs,
        out_specs=pl.BlockSpec((rows, hb * MLA_V), lambda b, h, i: (b * nq + i, h)),
        out_shape=jax.ShapeDtypeStruct((t, heads * MLA_V), _BF),
        scratch_shapes=[pltpu.VMEM((seq + past, tq), _F32)] * 2,
        compiler_params=_params("parallel", "parallel", "parallel"),
        name="mla_attn",
    )(*args)


def _fourier_kernel(cs_ref, sn_ref, z_ref, cc_ref, sc_ref, w_ref, o_ref, *, norm):
    z = z_ref[...]
    a = _dot(cs_ref[...], z)
    b = _dot(sn_ref[...], z)
    gc = FNET_GC
    for g in range(FNET_GROUPS):
        f = _dot(a[:, g * gc:(g + 1) * gc].astype(_BF), cc_ref[...]) - _dot(b[:, g * gc:(g + 1) * gc].astype(_BF), sc_ref[...])
        f = (f * norm).astype(_BF)
        o_ref[:, g * gc:(g + 1) * gc] = _dot(f, w_ref[g]).astype(_BF)


def _dft_tables(n):
    k = lax.broadcasted_iota(jnp.int32, (n, n), 0) * lax.broadcasted_iota(jnp.int32, (n, n), 1) % n
    ang = k.astype(_F32) * (2.0 * math.pi / n)
    return jnp.cos(ang).astype(_BF), jnp.sin(ang).astype(_BF)


def _fourier_call(z, tabs_s, tabs_c, w, batch, seq):
    t, fw = z.shape
    tq = min(TOKEN_TILE, seq)
    nq = seq // tq
    cs, sn = tabs_s
    cc, sc = tabs_c
    gc = FNET_GC
    return pl.pallas_call(
        functools.partial(_fourier_kernel, norm=1.0 / math.sqrt(seq * gc)),
        grid=(nq, batch),
        in_specs=[pl.BlockSpec((tq, seq), lambda i, b: (i, 0)),
                  pl.BlockSpec((tq, seq), lambda i, b: (i, 0)),
                  pl.BlockSpec((seq, fw), lambda i, b: (b, 0)),
                  pl.BlockSpec((gc, gc), lambda i, b: (0, 0)),
                  pl.BlockSpec((gc, gc), lambda i, b: (0, 0)),
                  pl.BlockSpec(w.shape, lambda i, b: (0, 0, 0))],
        out_specs=pl.BlockSpec((tq, fw), lambda i, b: (b * nq + i, 0)),
        out_shape=jax.ShapeDtypeStruct((t, fw), _BF),
        compiler_params=_params("parallel", "parallel"),
        name="fourier",
    )(cs, sn, z, cc, sc, w)


def _proj_d_kernel(*refs, rope, state, npz, nqk, chunks):
    x_ref, m_ref, g_ref, w_ref = refs[:4]
    refs = refs[4:]
    if rope:
        c_ref, sa_ref, sb_ref = refs[:3]
        refs = refs[3:]
    pz_ref, qk_ref, vt_ref = refs[:3]
    h_ref = refs[-1]
    j = pl.program_id(1)

    @pl.when(j == 0)
    def _():
        h_ref[...] = _norm_mod(x_ref[...], g_ref[...], m_ref[3:4, :], m_ref[4:5, :]).astype(_BF)

    acc = _dot(h_ref[...], w_ref[...])

    @pl.when(j < npz)
    def _():
        pz_ref[...] = acc

    @pl.when((j >= npz) & (j < npz + 2 * nqk))
    def _():
        if rope:
            c, sa, sb = c_ref[...], sa_ref[...], sb_ref[...]
            for k in range(chunks):
                qk_ref[:, k * LANES:(k + 1) * LANES] = _rope(acc[:, k * LANES:(k + 1) * LANES], c, sa, sb).astype(_BF)
        else:
            qk_ref[...] = acc.astype(_BF)

    @pl.when(j >= npz + 2 * nqk)
    def _():
        vt_ref[...] = acc.T.astype(_BF)

    if state:
        @pl.when(j >= npz + nqk)
        def _():
            refs[3][...] = acc


def _proj_d_call(x, mod, g, w_in, tables, layer, tm, tpb, state):
    t, d = x.shape
    n = w_in.shape[1]
    pw = len(POOL_WINDOWS) * POOL_GC
    qk = DIFF_HEADS * 2 * DIFF_DH
    tn = math.gcd(math.gcd(pw, qk), 512)
    npz, nqk = pw // tn, qk // tn
    rope = tables is not None
    row = _mod_row(layer, tpb)
    in_specs = [pl.BlockSpec((tm, d), lambda i, j: (i, 0)),
                pl.BlockSpec((None, N_MOD, d), lambda i, j: (row(i), 0, 0)),
                pl.BlockSpec((1, d), lambda i, j: (0, 0)),
                pl.BlockSpec((d, tn), lambda i, j: (0, j))]
    args = [x, mod, g, w_in]
    if rope:
        in_specs += [pl.BlockSpec((tm, LANES), lambda i, j: (i % tpb, 0))] * 3
        args += list(tables)
    out_specs = [pl.BlockSpec((tm, tn), lambda i, j: (i, jnp.minimum(j, npz - 1))),
                 pl.BlockSpec((tm, tn), lambda i, j: (i, jnp.clip(j - npz, 0, 2 * nqk - 1))),
                 pl.BlockSpec((tn, tm), lambda i, j: (jnp.maximum(j - npz - 2 * nqk, 0), i))]
    out_shape = [jax.ShapeDtypeStruct((t, pw), _F32), jax.ShapeDtypeStruct((t, 2 * qk), _BF),
                 jax.ShapeDtypeStruct((qk, t), _BF)]
    if state:
        out_specs.append(pl.BlockSpec((tm, tn), lambda i, j: (i, jnp.maximum(j - npz - nqk, 0))))
        out_shape.append(jax.ShapeDtypeStruct((t, 2 * qk), _F32))
    return pl.pallas_call(
        functools.partial(_proj_d_kernel, rope=rope, state=state, npz=npz, nqk=nqk, chunks=tn // LANES),
        grid=(t // tm, n // tn),
        in_specs=in_specs,
        out_specs=out_specs,
        out_shape=out_shape,
        scratch_shapes=[pltpu.VMEM((tm, d), _BF)],
        compiler_params=_params("parallel", "arbitrary"),
        name="proj_d",
    )(*args)


def _diff_kernel(*refs, lam_init, hb, nqt, tq, seq, past):
    if past:
        q_ref, k_ref, v_ref, kc_ref, vc_ref, lam_ref, g_ref, o_ref, *scr = refs
    else:
        q_ref, k_ref, v_ref, lam_ref, g_ref, o_ref, *scr = refs
        kc_ref = vc_ref = None
    s_refs, e_refs = (scr[0:2], scr[2:4]), scr[4:6]
    lq = lam_ref[...]
    lam = (jnp.exp(jnp.sum(lq[0:1] * lq[1:2], axis=-1, keepdims=True))
           - jnp.exp(jnp.sum(lq[2:3] * lq[3:4], axis=-1, keepdims=True)) + lam_init)
    hw = 2 * DIFF_DH
    lane = lax.broadcasted_iota(jnp.int32, (tq, hw), 1)
    chunks = _key_chunks(seq, past)
    units = [(hd, qt) for hd in range(hb) for qt in range(nqt)]
    col_max = [None] * len(units)
    col_sum = [None] * len(units)

    def scores(u):
        hd, qt = units[u]
        cols = slice(hd * hw, (hd + 1) * hw)
        q = q_ref[qt * tq:(qt + 1) * tq, cols]
        zero = jnp.zeros_like(q)
        halves = (jnp.where(lane < DIFF_DH, q, zero), jnp.where(lane >= DIFF_DH, q, zero))
        m = [None, None]
        for is_cache, off, n, row in chunks:
            kr = kc_ref if is_cache else k_ref
            k = kr[off:off + n, cols]
            for c in range(2):
                s = _dot_t(k, halves[c])
                s_refs[u % 2][c][row:row + n, :] = s
                mc = jnp.max(s, axis=0, keepdims=True)
                m[c] = mc if m[c] is None else jnp.maximum(m[c], mc)
            yield
        col_max[u] = m

    def exponentials(u):
        m = col_max[u]
        l = [None, None]
        for _, _, n, row in chunks:
            for c in range(2):
                e = jnp.exp2(s_refs[u % 2][c][row:row + n, :] - m[c])
                e_refs[c][row:row + n, :] = e
                lc = jnp.sum(e, axis=0, keepdims=True)
                l[c] = lc if l[c] is None else l[c] + lc
            yield
        col_sum[u] = l

    def combine_pv(u):
        hd, qt = units[u]
        cols = slice(hd * hw, (hd + 1) * hw)
        l0, l1 = col_sum[u]
        ratio = lam * l0 / l1
        acc = None
        for is_cache, off, n, row in chunks:
            vr = vc_ref if is_cache else v_ref
            a = (e_refs[0][row:row + n, :] - ratio * e_refs[1][row:row + n, :]).astype(_BF)
            oc = _dot(vr[cols, off:off + n], a)
            acc = oc if acc is None else acc + oc
            yield
        o = (acc / l0).T
        o_ref[qt * tq:(qt + 1) * tq, cols] = (_rms(o, g_ref[...]) * (1.0 - lam_init)).astype(_BF)

    _interleave(scores(0))
    for u in range(len(units)):
        if u + 1 < len(units):
            _interleave(scores(u + 1), exponentials(u))
        else:
            _interleave(exponentials(u))
        _interleave(combine_pv(u))


def _diff_call(qk, vt, kc, vct, lam_qk, g_sub, batch, seq, lam_init):
    t = qk.shape[0]
    heads = DIFF_HEADS
    hw = 2 * DIFF_DH
    hb, nqt, tq = _attn_units(seq, heads, DIFF_Q_TILE, DIFF_UNITS)
    rows = nqt * tq
    nq = seq // rows
    nh = heads // hb
    past = 0 if kc is None else kc.shape[0] // batch
    in_specs = [pl.BlockSpec((rows, hb * hw), lambda b, h, i: (b * nq + i, h)),
                pl.BlockSpec((seq, hb * hw), lambda b, h, i: (b, nh + h)),
                pl.BlockSpec((hb * hw, seq), lambda b, h, i: (h, b))]
    args = [qk, qk, vt]
    if past:
        in_specs += [pl.BlockSpec((past, hb * hw), lambda b, h, i: (b, h)),
                     pl.BlockSpec((hb * hw, past), lambda b, h, i: (h, b))]
        args += [kc, vct]
    in_specs += [pl.BlockSpec(lam_qk.shape, lambda b, h, i: (0, 0)),
                 pl.BlockSpec((1, hw), lambda b, h, i: (0, 0))]
    args += [lam_qk, g_sub]
    return pl.pallas_call(
        functools.partial(_diff_kernel, lam_init=lam_init, hb=hb, nqt=nqt, tq=tq, seq=seq, past=past),
        grid=(batch, nh, nq),
        in_specs=in_specs,
        out_specs=pl.BlockSpec((rows, hb * hw), lambda b, h, i: (b * nq + i, h)),
        out_shape=jax.ShapeDtypeStruct((t, heads * hw), _BF),
        scratch_shapes=[pltpu.VMEM((seq + past, tq), _F32)] * 6,
        compiler_params=_params("parallel", "parallel", "parallel"),
        name="diff_attn",
    )(*args)


POOL_PAD = 16


def _pool_kernel(pz_ref, w_ref, sc_ref, o_ref, pad_ref, *, seq):
    gc = POOL_GC
    t = lax.broadcasted_iota(jnp.int32, (seq, gc), 0)
    zeros = jnp.zeros((POOL_PAD, gc), _F32)
    pad_ref[0:POOL_PAD, :] = zeros
    pad_ref[POOL_PAD + seq:2 * POOL_PAD + seq, :] = zeros
    for g, w in enumerate(POOL_WINDOWS):
        x = pz_ref[:, g * gc:(g + 1) * gc]
        pad_ref[POOL_PAD:POOL_PAD + seq, :] = x
        lo, hi = w // 2, w - w // 2
        tot = pad_ref[POOL_PAD - lo:POOL_PAD - lo + seq, :]
        for dlt in range(-lo + 1, hi):
            tot = tot + pad_ref[POOL_PAD + dlt:POOL_PAD + dlt + seq, :]
        cnt = (jnp.minimum(t + hi, seq) - jnp.maximum(t - lo, 0)).astype(_F32)
        pooled = (tot / cnt - x).astype(_BF)
        y = _dot(pooled, w_ref[g]) * sc_ref[:, g * gc:(g + 1) * gc]
        o_ref[:, g * gc:(g + 1) * gc] = y.astype(_BF)


def _pool_call(pz, w_pool, pool_scale, batch, seq):
    t, pw = pz.shape
    assert max(POOL_WINDOWS) <= POOL_PAD
    return pl.pallas_call(
        functools.partial(_pool_kernel, seq=seq),
        grid=(batch,),
        in_specs=[pl.BlockSpec((seq, pw), lambda b: (b, 0)),
                  pl.BlockSpec(w_pool.shape, lambda b: (0, 0, 0)),
                  pl.BlockSpec((1, pw), lambda b: (0, 0))],
        out_specs=pl.BlockSpec((seq, pw), lambda b: (b, 0)),
        out_shape=jax.ShapeDtypeStruct((t, pw), _BF),
        scratch_shapes=[pltpu.VMEM((seq + 2 * POOL_PAD, POOL_GC), _F32)],
        compiler_params=_params("parallel"),
        name="pool",
    )(pz, w_pool, pool_scale)


def _rope_tables(seq, reps, pad):
    pos = jnp.arange(seq)
    half = MLA_ROPE // 4
    inv = ROPE_BASE ** (-jnp.arange(half, dtype=_F32) / half)
    ang_r = (pos // GRID_W).astype(_F32)[:, None] * inv
    ang_c = (pos % GRID_W).astype(_F32)[:, None] * inv
    cos = jnp.concatenate([jnp.cos(ang_r)] * 2 + [jnp.cos(ang_c)] * 2, axis=-1)
    sin_r, sin_c = jnp.sin(ang_r), jnp.sin(ang_c)
    z = jnp.zeros_like(sin_r)
    sa = jnp.concatenate([-sin_r, z, -sin_c, z], axis=-1)
    sb = jnp.concatenate([z, sin_r, z, sin_c], axis=-1)
    cos, sa, sb = (jnp.tile(a, (1, reps)) for a in (cos, sa, sb))
    if pad:
        cos = jnp.concatenate([cos, jnp.ones((seq, pad), _F32)], axis=-1)
        sa = jnp.concatenate([sa, jnp.zeros((seq, pad), _F32)], axis=-1)
        sb = jnp.concatenate([sb, jnp.zeros((seq, pad), _F32)], axis=-1)
    return cos, sa, sb


def _prep_weights(w_in_a, w_q_up, w_kv_up, w_in_d):
    na, d, _ = w_in_a.shape
    ql, kl, fw = Q_LORA, KV_LORA, FNET_GROUPS * FNET_GC
    q, ckv, kr, fz = (w_in_a[..., :ql], w_in_a[..., ql:ql + kl], w_in_a[..., ql + kl:ql + kl + MLA_ROPE],
                      w_in_a[..., ql + kl + MLA_ROPE:])
    w_in = jnp.concatenate([q, ckv, fz, kr, jnp.zeros((na, d, LANES - MLA_ROPE), _F32)], axis=-1).astype(_BF)
    scale = LOG2_E / math.sqrt(MLA_NOPE + MLA_ROPE)
    wq = (w_q_up * scale).reshape(na, ql, MLA_HEADS, MLA_NOPE + MLA_ROPE)
    wq = jnp.pad(wq, ((0, 0), (0, 0), (0, 0), (0, LANES - MLA_ROPE))).reshape(na, ql, -1).astype(_BF)
    wkv = w_kv_up.astype(_BF)
    pw, qk = len(POOL_WINDOWS) * POOL_GC, DIFF_HEADS * 2 * DIFF_DH
    wd = jnp.concatenate([w_in_d[..., :pw], w_in_d[..., pw:pw + qk] * (LOG2_E / math.sqrt(DIFF_DH)),
                          w_in_d[..., pw + qk:]], axis=-1).astype(_BF)
    return w_in, wq, wkv, wd


def _trunk(x, mod, batch, seq, caches, p, tpb_of):
    t, d = x.shape
    depth = p["g_norm"].shape[0]
    decode = caches is not None
    tm = min(TOKEN_TILE, seq) if decode else min(TOKEN_TILE, t)
    tpb = seq // tm if decode else None
    states = ([], [], [], [])
    if decode:
        tab_mla = _rope_tables(seq, 1, LANES - MLA_ROPE)
        tab_diff = _rope_tables(seq, LANES // DIFF_DH, 0)
    else:
        tab_mla = tab_diff = None
    dft_s = _dft_tables(seq)
    dft_c = _dft_tables(FNET_GC)
    aw = MLA_HEADS * MLA_V
    pw = len(POOL_WINDOWS) * POOL_GC
    qk = DIFF_HEADS * 2 * DIFF_DH
    for l in range(depth):
        i = l // 2
        g = p["g_norm"][l]
        x = _ffn_call(x, mod, g[0:1], p["wg"], p["wu"], p["wd"], l, 0, tm, tpb)
        if l % 2 == 0:
            outs = _proj_a_call(x, mod, g[1:2], p["w_in_a"][i], p["g_q"][i:i + 1], p["g_kv"][i:i + 1], p["wq"][i],
                                tab_mla, l, tm, tpb, not decode)
            q, ckv, kr, fz = outs[:4]
            if not decode:
                states[0].append(outs[4])
                states[1].append(outs[5][:, :MLA_ROPE])
            k, v = _kvup_call(ckv, kr, p["wkv"][i], tm)
            kc = vc = None
            if decode:
                past = caches[0].shape[2]
                c_ckv = caches[0][:, i].reshape(batch * past, KV_LORA).astype(_BF)
                c_kr = jnp.pad(caches[1][:, i].reshape(batch * past, MLA_ROPE), ((0, 0), (0, LANES - MLA_ROPE))).astype(_BF)
                kc, vc = _kvup_call(c_ckv, c_kr, p["wkv"][i], min(TOKEN_TILE, past))
            attn = _mla_call(q, k, v, kc, vc, batch, seq)
            four = _fourier_call(fz, dft_s, dft_c, p["w_fnet"][i], batch, seq)
            x = _oproj_call(attn, four, p["w_o_a"][i, :aw], p["w_o_a"][i, aw:], x, mod, l, tm, tpb)
        else:
            lam_init = 0.8 - 0.6 * math.exp(-0.3 * l)
            tm_d = min(PROJ_D_TILE, seq) if decode else tm
            outs = _proj_d_call(x, mod, g[1:2], p["w_in_d"][i], tab_diff, l, tm_d, seq // tm_d if decode else None,
                                not decode)
            pz, qk_new, vt_new = outs[:3]
            if not decode:
                states[2].append(outs[3][:, :qk])
                states[3].append(outs[3][:, qk:])
            kc = vct = None
            if decode:
                past = caches[2].shape[2]
                kc = caches[2][:, i].reshape(batch * past, qk).astype(_BF)
                vct = caches[3][:, i].reshape(batch * past, qk).T.astype(_BF)
            o = _diff_call(qk_new, vt_new, kc, vct, p["lam_qk"][i], p["g_sub"][i:i + 1], batch, seq, lam_init)
            pool = _pool_call(pz, p["w_pool"][i], p["pool_scale"][i:i + 1], batch, seq)
            x = _oproj_call(pool, o, p["w_o_d"][i, :pw], p["w_o_d"][i, pw:], x, mod, l, tm, tpb)
        x = _ffn_call(x, mod, g[2:3], p["wg"], p["wu"], p["wd"], l, 1, tm, tpb,
                      g_final=p["g_final"] if l == depth - 1 else None)
    return x, states


def kernel(x_prompt, x_sample, cache_mla_ckv, cache_mla_krope, cache_diff_k, cache_diff_v, c, c_ctx, w_mod, b_mod, g_norm, w_ffn_gate, w_ffn_up, w_ffn_down, w_in_a, g_q, g_kv, w_q_up, w_kv_up, w_fnet, w_o_a, w_in_d, lam_qk, g_sub, w_pool, pool_scale, w_o_d, g_final):
    batch, seq, d = x_prompt.shape
    dbatch, dseq, _ = x_sample.shape
    assert 1 + dbatch <= MOD_ROWS
    cond = jnp.concatenate([c_ctx[None, :], c, jnp.zeros((MOD_ROWS - 1 - dbatch, d), _F32)], axis=0)
    mod = _mod_call(cond, w_mod, b_mod)

    w_in, wq, wkv, wd_in = _prep_weights(w_in_a, w_q_up, w_kv_up, w_in_d)
    p = dict(g_norm=g_norm, wg=w_ffn_gate.astype(_BF), wu=w_ffn_up.astype(_BF), wd=w_ffn_down.astype(_BF),
             w_in_a=w_in, g_q=g_q, g_kv=g_kv, wq=wq, wkv=wkv, w_fnet=w_fnet.astype(_BF), w_o_a=w_o_a.astype(_BF),
             w_in_d=wd_in, lam_qk=lam_qk, g_sub=g_sub, w_pool=w_pool.astype(_BF), pool_scale=pool_scale,
             w_o_d=w_o_d.astype(_BF), g_final=g_final[None, :])

    y_p, st = _trunk(x_prompt.reshape(batch * seq, d), mod, batch, seq, None, p, None)
    caches = (cache_mla_ckv, cache_mla_krope, cache_diff_k, cache_diff_v)
    y_s, _ = _trunk(x_sample.reshape(dbatch * dseq, d), mod, dbatch, dseq, caches, p, None)

    new_ckv = jnp.stack([s.reshape(batch, seq, KV_LORA) for s in st[0]], axis=1)
    new_kr = jnp.stack([s.reshape(batch, seq, MLA_ROPE) for s in st[1]], axis=1)
    new_k = jnp.stack([s.reshape(batch, seq, DIFF_HEADS, 2, DIFF_DH) for s in st[2]], axis=1)
    new_v = jnp.stack([s.reshape(batch, seq, DIFF_HEADS, 2 * DIFF_DH) for s in st[3]], axis=1)
    return (y_p.reshape(batch, seq, d), y_s.reshape(dbatch, dseq, d), new_ckv, new_kr, new_k, new_v)
```

```python
import functools
import math

import jax
import jax.numpy as jnp
from jax import lax
from jax.experimental import pallas as pl
from jax.experimental.pallas import tpu as pltpu

GRID_W = 64
MLA_HEADS = 12
MLA_NOPE = 128
MLA_ROPE = 64
MLA_V = 128
Q_LORA = 512
KV_LORA = 512
FNET_GROUPS = 4
FNET_GC = 128
POOL_WINDOWS = (2, 4, 8, 16)
POOL_GC = 128
DIFF_HEADS = 12
DIFF_DH = 64
N_MOD = 9
ROPE_BASE = 10000.0
EPS = 1e-6
LOG2_E = math.log2(math.e)

LANES = 128
MXU_DEPTH = 256
VMEM_BYTES = 64 * 2 ** 20
VMEM_LIMIT = VMEM_BYTES - 8 * 2 ** 20

TOKEN_TILE = 512
FFN_TILE = 512
PROJ_D_ROW_CHUNK = 256
PROJ_D_TILE = 1024
MLA_Q_TILE = 256
MLA_UNITS = 8
DIFF_Q_TILE = 256
DIFF_UNITS = 8
ATTN_KEY_CHUNK = 512
MOD_ROWS = 16

_BF = jnp.bfloat16
_F32 = jnp.float32


def _tile(n, pref):
    if n <= pref:
        return n
    t = (pref // LANES) * LANES
    while n % t:
        t -= LANES
    return t


def _params(*sem):
    return pltpu.CompilerParams(dimension_semantics=sem, vmem_limit_bytes=VMEM_LIMIT)


def _dot(a, b):
    return jnp.dot(a, b, preferred_element_type=_F32)


def _dot_t(a, b):
    return lax.dot_general(a, b, (((1,), (1,)), ((), ())), preferred_element_type=_F32)


def _rms(x, g):
    ms = jnp.mean(x * x, axis=-1, keepdims=True)
    return x * lax.rsqrt(ms + EPS) * g


def _norm_mod(x, g, shift, scale):
    return _rms(x, g) * (1.0 + scale) + shift


def _silu(x):
    return x / (1.0 + jnp.exp(-x))


def _rope(x, c, sa, sb):
    return x * c + pltpu.roll(x, LANES - 16, 1) * sa + pltpu.roll(x, 16, 1) * sb


def _mod_row(layer, tiles_per_batch):
    if tiles_per_batch is None:
        return lambda i: layer * MOD_ROWS
    return lambda i: layer * MOD_ROWS + 1 + i // tiles_per_batch


def _mod_kernel(c_ref, w_ref, b_ref, o_ref):
    s = _silu(c_ref[...]).astype(_BF)
    o_ref[...] = _dot(s, w_ref[...].astype(_BF)) + b_ref[...]


def _mod_call(cond, w_mod, b_mod):
    depth, d, n = w_mod.shape
    r = cond.shape[0]
    tn = _tile(n, 1024)
    out = pl.pallas_call(
        _mod_kernel,
        grid=(depth, n // tn),
        in_specs=[pl.BlockSpec((r, d), lambda l, j: (0, 0)),
                  pl.BlockSpec((None, d, tn), lambda l, j: (l, 0, j)),
                  pl.BlockSpec((None, 1, tn), lambda l, j: (l, 0, j))],
        out_specs=pl.BlockSpec((None, r, tn), lambda l, j: (l, 0, j)),
        out_shape=jax.ShapeDtypeStruct((depth, r, n), _F32),
        compiler_params=_params("parallel", "parallel"),
        name="mod",
    )(cond, w_mod, b_mod.reshape(depth, 1, n))
    return out.reshape(depth * r, N_MOD, d)


def _ffn_kernel(x_ref, m_ref, g_ref, wg_ref, wu_ref, wd_ref, gf_ref, o_ref, h_ref, a_ref, acc_ref, *, base, final, nf):
    j = pl.program_id(1)
    last = nf

    def activation():
        h = h_ref[...]
        return (_silu(_dot(h, wg_ref[...])) * _dot(h, wu_ref[...])).astype(_BF)

    @pl.when(j == 0)
    def _():
        h = _norm_mod(x_ref[...], g_ref[...], m_ref[base:base + 1, :], m_ref[base + 1:base + 2, :])
        h_ref[...] = h.astype(_BF)
        a_ref[0] = activation()

    if nf > 1:
        @pl.when(j == 1)
        def _():
            a_ref[1] = activation()
            acc_ref[...] = _dot(a_ref[0], wd_ref[...])

    @pl.when((j > 1) & (j < last))
    def _():
        a_ref[j % 2] = activation()
        acc_ref[...] += _dot(a_ref[(j + 1) % 2], wd_ref[...])

    @pl.when(j == last)
    def _():
        acc = _dot(a_ref[(nf + 1) % 2], wd_ref[...])
        if nf > 1:
            acc = acc_ref[...] + acc
        y = x_ref[...] + (0.5 * m_ref[base + 2:base + 3, :]) * acc
        if final:
            y = _rms(y, gf_ref[...])
        o_ref[...] = y


def _ffn_call(x, mod, g, wg, wu, wd, layer, sub, tm, tpb, g_final=None):
    t, d = x.shape
    f = wg.shape[-1]
    tf = _tile(f, FFN_TILE)
    nf = f // tf
    row = _mod_row(layer, tpb)
    final = g_final is not None
    gf = g_final if final else g
    return pl.pallas_call(
        functools.partial(_ffn_kernel, base=0 if sub == 0 else 6, final=final, nf=nf),
        grid=(t // tm, nf + 1),
        in_specs=[pl.BlockSpec((tm, d), lambda i, j: (i, 0)),
                  pl.BlockSpec((None, N_MOD, d), lambda i, j: (row(i), 0, 0)),
                  pl.BlockSpec((1, d), lambda i, j: (0, 0)),
                  pl.BlockSpec((None, None, d, tf), lambda i, j: (layer, sub, 0, jnp.minimum(j, nf - 1))),
                  pl.BlockSpec((None, None, d, tf), lambda i, j: (layer, sub, 0, jnp.minimum(j, nf - 1))),
                  pl.BlockSpec((None, None, tf, d), lambda i, j: (layer, sub, jnp.maximum(j - 1, 0), 0)),
                  pl.BlockSpec((1, d), lambda i, j: (0, 0))],
        out_specs=pl.BlockSpec((tm, d), lambda i, j: (i, 0)),
        out_shape=jax.ShapeDtypeStruct((t, d), _F32),
        scratch_shapes=[pltpu.VMEM((tm, d), _BF), pltpu.VMEM((2, tm, tf), _BF), pltpu.VMEM((tm, d), _F32)],
        compiler_params=_params("parallel", "arbitrary"),
        name="ffn",
    )(x, mod, g, wg, wu, wd, gf)


def _oproj_kernel(a1_ref, a2_ref, w1_ref, w2_ref, x_ref, m_ref, o_ref):
    acc = _dot(a1_ref[...], w1_ref[...]) + _dot(a2_ref[...], w2_ref[...])
    o_ref[...] = x_ref[...] + m_ref[5:6, :] * acc


def _oproj_call(a1, a2, w1, w2, x, mod, layer, tm, tpb):
    t, d = x.shape
    k1, k2 = a1.shape[1], a2.shape[1]
    row = _mod_row(layer, tpb)
    return pl.pallas_call(
        _oproj_kernel,
        grid=(t // tm,),
        in_specs=[pl.BlockSpec((tm, k1), lambda i: (i, 0)),
                  pl.BlockSpec((tm, k2), lambda i: (i, 0)),
                  pl.BlockSpec((k1, d), lambda i: (0, 0)),
                  pl.BlockSpec((k2, d), lambda i: (0, 0)),
                  pl.BlockSpec((tm, d), lambda i: (i, 0)),
                  pl.BlockSpec((None, N_MOD, d), lambda i: (row(i), 0, 0))],
        out_specs=pl.BlockSpec((tm, d), lambda i: (i, 0)),
        out_shape=jax.ShapeDtypeStruct((t, d), _F32),
        compiler_params=_params("parallel"),
        name="oproj",
    )(a1, a2, w1, w2, x, mod)


def _proj_a_kernel(*refs, rope, state, heads):
    x_ref, m_ref, g_ref, win_ref, gq_ref, gkv_ref, wq_ref = refs[:7]
    refs = refs[7:]
    if rope:
        c_ref, sa_ref, sb_ref = refs[:3]
        refs = refs[3:]
    q_ref, ckv_ref, kr_ref, fz_ref = refs[:4]
    h = _norm_mod(x_ref[...], g_ref[...], m_ref[3:4, :], m_ref[4:5, :]).astype(_BF)
    u = _dot(h, win_ref[...])
    ql, kl = Q_LORA, KV_LORA
    fw = FNET_GROUPS * FNET_GC
    qn = _rms(u[:, :ql], gq_ref[...]).astype(_BF)
    ckv = _rms(u[:, ql:ql + kl], gkv_ref[...])
    kr = u[:, ql + kl + fw:]
    ckv_ref[...] = ckv.astype(_BF)
    fz_ref[...] = u[:, ql + kl:ql + kl + fw].astype(_BF)
    if state:
        refs[4][...] = ckv
        refs[5][...] = kr
    q = _dot(qn, wq_ref[...])
    hw = MLA_NOPE + LANES
    if rope:
        c, sa, sb = c_ref[...], sa_ref[...], sb_ref[...]
        kr = _rope(kr, c, sa, sb)
        for hd in range(heads):
            q_ref[:, hd * hw:hd * hw + MLA_NOPE] = q[:, hd * hw:hd * hw + MLA_NOPE].astype(_BF)
            q_ref[:, hd * hw + MLA_NOPE:(hd + 1) * hw] = _rope(q[:, hd * hw + MLA_NOPE:(hd + 1) * hw], c, sa, sb).astype(_BF)
    else:
        q_ref[...] = q.astype(_BF)
    kr_ref[...] = kr.astype(_BF)


def _proj_a_call(x, mod, g, w_in, g_q, g_kv, wq, tables, layer, tm, tpb, state):
    t, d = x.shape
    n_in = w_in.shape[1]
    nq = wq.shape[1]
    fw = FNET_GROUPS * FNET_GC
    rope = tables is not None
    row = _mod_row(layer, tpb)
    in_specs = [pl.BlockSpec((tm, d), lambda i: (i, 0)),
                pl.BlockSpec((None, N_MOD, d), lambda i: (row(i), 0, 0)),
                pl.BlockSpec((1, d), lambda i: (0, 0)),
                pl.BlockSpec((d, n_in), lambda i: (0, 0)),
                pl.BlockSpec((1, Q_LORA), lambda i: (0, 0)),
                pl.BlockSpec((1, KV_LORA), lambda i: (0, 0)),
                pl.BlockSpec((Q_LORA, nq), lambda i: (0, 0))]
    args = [x, mod, g, w_in, g_q, g_kv, wq]
    if rope:
        in_specs += [pl.BlockSpec((tm, LANES), lambda i: (i % tpb, 0))] * 3
        args += list(tables)
    shapes = [((t, nq), _BF), ((t, KV_LORA), _BF), ((t, LANES), _BF), ((t, fw), _BF)]
    if state:
        shapes += [((t, KV_LORA), _F32), ((t, LANES), _F32)]
    return pl.pallas_call(
        functools.partial(_proj_a_kernel, rope=rope, state=state, heads=MLA_HEADS),
        grid=(t // tm,),
        in_specs=in_specs,
        out_specs=[pl.BlockSpec((tm, s[1]), lambda i: (i, 0)) for s, _ in shapes],
        out_shape=[jax.ShapeDtypeStruct(s, dt) for s, dt in shapes],
        compiler_params=_params("parallel"),
        name="proj_a",
    )(*args)


def _kvup_kernel(ckv_ref, kr_ref, w_ref, k_ref, v_ref, *, heads):
    kv = _dot(ckv_ref[...], w_ref[...])
    kr = kr_ref[...]
    hw = MLA_NOPE + MLA_V
    for hd in range(heads):
        k_ref[hd, :, :MLA_NOPE] = kv[:, hd * hw:hd * hw + MLA_NOPE].astype(_BF)
        k_ref[hd, :, MLA_NOPE:] = kr
        v_ref[hd] = kv[:, hd * hw + MLA_NOPE:(hd + 1) * hw].T.astype(_BF)


def _kvup_call(ckv, kr, wkv, tm):
    t = ckv.shape[0]
    heads = MLA_HEADS
    return pl.pallas_call(
        functools.partial(_kvup_kernel, heads=heads),
        grid=(t // tm,),
        in_specs=[pl.BlockSpec((tm, KV_LORA), lambda i: (i, 0)),
                  pl.BlockSpec((tm, LANES), lambda i: (i, 0)),
                  pl.BlockSpec(wkv.shape, lambda i: (0, 0))],
        out_specs=[pl.BlockSpec((heads, tm, MLA_NOPE + LANES), lambda i: (0, i, 0)),
                   pl.BlockSpec((heads, MLA_V, tm), lambda i: (0, 0, i))],
        out_shape=[jax.ShapeDtypeStruct((heads, t, MLA_NOPE + LANES), _BF),
                   jax.ShapeDtypeStruct((heads, MLA_V, t), _BF)],
        compiler_params=_params("parallel"),
        name="kvup",
    )(ckv, kr, wkv)


def _interleave(*stages):
    live = list(stages)
    while live:
        for st in list(live):
            try:
                next(st)
            except StopIteration:
                live.remove(st)


def _key_chunks(seq, past):
    kc = min(ATTN_KEY_CHUNK, seq)
    chunks = [(False, c * kc, kc, c * kc) for c in range(seq // kc)]
    if past:
        pc = min(ATTN_KEY_CHUNK, past)
        chunks += [(True, c * pc, pc, seq + c * pc) for c in range(past // pc)]
    return chunks


def _attn_units(seq, heads, q_tile, n_units):
    tq = min(q_tile, seq)
    nqt = min(n_units, seq // tq)
    hb = math.gcd(heads, max(n_units // nqt, 1))
    return hb, nqt, tq


def _mla_kernel(*refs, hb, nqt, tq, seq, past):
    if past:
        q_ref, k_ref, v_ref, kc_ref, vc_ref, o_ref, *s_refs = refs
    else:
        q_ref, k_ref, v_ref, o_ref, *s_refs = refs
        kc_ref = vc_ref = None
    qw = MLA_NOPE + LANES
    chunks = _key_chunks(seq, past)
    units = [(hd, qt) for hd in range(hb) for qt in range(nqt)]
    col_max = [None] * len(units)

    def scores(u):
        hd, qt = units[u]
        q = q_ref[qt * tq:(qt + 1) * tq, hd * qw:(hd + 1) * qw]
        m = None
        for is_cache, off, n, row in chunks:
            kr = kc_ref if is_cache else k_ref
            s = _dot_t(kr[hd, off:off + n, :], q)
            s_refs[u % 2][row:row + n, :] = s
            mc = jnp.max(s, axis=0, keepdims=True)
            m = mc if m is None else jnp.maximum(m, mc)
            yield
        col_max[u] = m

    def softmax_pv(u):
        hd, qt = units[u]
        m = col_max[u]
        l = acc = None
        for is_cache, off, n, row in chunks:
            vr = vc_ref if is_cache else v_ref
            p = jnp.exp2(s_refs[u % 2][row:row + n, :] - m)
            lc = jnp.sum(p, axis=0, keepdims=True)
            oc = _dot(vr[hd, :, off:off + n], p.astype(_BF))
            l = lc if l is None else l + lc
            acc = oc if acc is None else acc + oc
            yield
        o_ref[qt * tq:(qt + 1) * tq, hd * MLA_V:(hd + 1) * MLA_V] = (acc / l).T.astype(_BF)

    _interleave(scores(0))
    for u in range(len(units)):
        if u + 1 < len(units):
            _interleave(scores(u + 1), softmax_pv(u))
        else:
            _interleave(softmax_pv(u))


def _mla_call(q, k, v, kc, vc, batch, seq):
    t = q.shape[0]
    heads = MLA_HEADS
    hb, nqt, tq = _attn_units(seq, heads, MLA_Q_TILE, MLA_UNITS)
    rows = nqt * tq
    nq = seq // rows
    qw = MLA_NOPE + LANES
    past = 0 if kc is None else kc.shape[1] // batch
    in_specs = [pl.BlockSpec((rows, hb * qw), lambda b, h, i: (b * nq + i, h)),
                pl.BlockSpec((hb, seq, qw), lambda b, h, i: (h, b, 0)),
                pl.BlockSpec((hb, MLA_V, seq), lambda b, h, i: (h, 0, b))]
    args = [q, k, v]
    if past:
        in_specs += [pl.BlockSpec((hb, past, qw), lambda b, h, i: (h, b, 0)),
                     pl.BlockSpec((hb, MLA_V, past), lambda b, h, i: (h, 0, b))]
        args += [kc, vc]
    return pl.pallas_call(
        functools.partial(_mla_kernel, hb=hb, nqt=nqt, tq=tq, seq=seq, past=past),
        grid=(batch, heads // hb, nq),
        in_specs=in_specs,
        out_specs=pl.BlockSpec((rows, hb * MLA_V), lambda b, h, i: (b * nq + i, h)),
        out_shape=jax.ShapeDtypeStruct((t, heads * MLA_V), _BF),
        scratch_shapes=[pltpu.VMEM((seq + past, tq), _F32)] * 2,
        compiler_params=_params("parallel", "parallel", "parallel"),
        name="mla_attn",
    )(*args)


def _fourier_kernel(cs_ref, sn_ref, z_ref, cc_ref, sc_ref, w_ref, o_ref, *, norm):
    z = z_ref[...]
    a = _dot(cs_ref[...], z)
    b = _dot(sn_ref[...], z)
    gc = FNET_GC
    for g in range(FNET_GROUPS):
        f = _dot(a[:, g * gc:(g + 1) * gc].astype(_BF), cc_ref[...]) - _dot(b[:, g * gc:(g + 1) * gc].astype(_BF), sc_ref[...])
        f = (f * norm).astype(_BF)
        o_ref[:, g * gc:(g + 1) * gc] = _dot(f, w_ref[g]).astype(_BF)


def _dft_tables(n):
    k = lax.broadcasted_iota(jnp.int32, (n, n), 0) * lax.broadcasted_iota(jnp.int32, (n, n), 1) % n
    ang = k.astype(_F32) * (2.0 * math.pi / n)
    return jnp.cos(ang).astype(_BF), jnp.sin(ang).astype(_BF)


def _fourier_call(z, tabs_s, tabs_c, w, batch, seq):
    t, fw = z.shape
    tq = min(TOKEN_TILE, seq)
    nq = seq // tq
    cs, sn = tabs_s
    cc, sc = tabs_c
    gc = FNET_GC
    return pl.pallas_call(
        functools.partial(_fourier_kernel, norm=1.0 / math.sqrt(seq * gc)),
        grid=(nq, batch),
        in_specs=[pl.BlockSpec((tq, seq), lambda i, b: (i, 0)),
                  pl.BlockSpec((tq, seq), lambda i, b: (i, 0)),
                  pl.BlockSpec((seq, fw), lambda i, b: (b, 0)),
                  pl.BlockSpec((gc, gc), lambda i, b: (0, 0)),
                  pl.BlockSpec((gc, gc), lambda i, b: (0, 0)),
                  pl.BlockSpec(w.shape, lambda i, b: (0, 0, 0))],
        out_specs=pl.BlockSpec((tq, fw), lambda i, b: (b * nq + i, 0)),
        out_shape=jax.ShapeDtypeStruct((t, fw), _BF),
        compiler_params=_params("parallel", "parallel"),
        name="fourier",
    )(cs, sn, z, cc, sc, w)


def _proj_d_kernel(*refs, rope, state, npz, nqk, chunks):
    x_ref, m_ref, g_ref, w_ref = refs[:4]
    refs = refs[4:]
    if rope:
        c_ref, sa_ref, sb_ref = refs[:3]
        refs = refs[3:]
    pz_ref, qk_ref, vt_ref = refs[:3]
    h_ref = refs[-1]
    j = pl.program_id(1)

    @pl.when(j == 0)
    def _():
        h_ref[...] = _norm_mod(x_ref[...], g_ref[...], m_ref[3:4, :], m_ref[4:5, :]).astype(_BF)

    tm = h_ref.shape[0]
    rc = min(PROJ_D_ROW_CHUNK, tm)
    row_chunks = [slice(r, r + rc) for r in range(0, tm, rc)]

    @pl.when(j < npz)
    def _():
        for rows in row_chunks:
            pz_ref[rows, :] = _dot(h_ref[rows, :], w_ref[...])

    def rotary_tiles(keep):
        for rows in row_chunks:
            acc = _dot(h_ref[rows, :], w_ref[...])
            if keep:
                refs[3][rows, :] = acc
            if rope:
                c, sa, sb = c_ref[rows, :], sa_ref[rows, :], sb_ref[rows, :]
                for k in range(chunks):
                    cols = slice(k * LANES, (k + 1) * LANES)
                    qk_ref[rows, cols] = _rope(acc[:, cols], c, sa, sb).astype(_BF)
            else:
                qk_ref[rows, :] = acc.astype(_BF)

    @pl.when((j >= npz) & (j < npz + nqk))
    def _():
        rotary_tiles(False)

    @pl.when((j >= npz + nqk) & (j < npz + 2 * nqk))
    def _():
        rotary_tiles(state)

    @pl.when(j >= npz + 2 * nqk)
    def _():
        for rows in row_chunks:
            acc = _dot(h_ref[rows, :], w_ref[...])
            if state:
                refs[3][rows, :] = acc
            vt_ref[:, rows] = acc.T.astype(_BF)


def _proj_d_call(x, mod, g, w_in, tables, layer, tm, tpb, state):
    t, d = x.shape
    n = w_in.shape[1]
    pw = len(POOL_WINDOWS) * POOL_GC
    qk = DIFF_HEADS * 2 * DIFF_DH
    tn = math.gcd(math.gcd(pw, qk), 512)
    npz, nqk = pw // tn, qk // tn
    rope = tables is not None
    row = _mod_row(layer, tpb)
    in_specs = [pl.BlockSpec((tm, d), lambda i, j: (i, 0)),
                pl.BlockSpec((None, N_MOD, d), lambda i, j: (row(i), 0, 0)),
                pl.BlockSpec((1, d), lambda i, j: (0, 0)),
                pl.BlockSpec((d, tn), lambda i, j: (0, j))]
    args = [x, mod, g, w_in]
    if rope:
        in_specs += [pl.BlockSpec((tm, LANES), lambda i, j: (i % tpb, 0))] * 3
        args += list(tables)
    out_specs = [pl.BlockSpec((tm, tn), lambda i, j: (i, jnp.minimum(j, npz - 1))),
                 pl.BlockSpec((tm, tn), lambda i, j: (i, jnp.clip(j - npz, 0, 2 * nqk - 1))),
                 pl.BlockSpec((tn, tm), lambda i, j: (jnp.maximum(j - npz - 2 * nqk, 0), i))]
    out_shape = [jax.ShapeDtypeStruct((t, pw), _F32), jax.ShapeDtypeStruct((t, 2 * qk), _BF),
                 jax.ShapeDtypeStruct((qk, t), _BF)]
    if state:
        out_specs.append(pl.BlockSpec((tm, tn), lambda i, j: (i, jnp.maximum(j - npz - nqk, 0))))
        out_shape.append(jax.ShapeDtypeStruct((t, 2 * qk), _F32))
    return pl.pallas_call(
        functools.partial(_proj_d_kernel, rope=rope, state=state, npz=npz, nqk=nqk, chunks=tn // LANES),
        grid=(t // tm, n // tn),
        in_specs=in_specs,
        out_specs=out_specs,
        out_shape=out_shape,
        scratch_shapes=[pltpu.VMEM((tm, d), _BF)],
        compiler_params=_params("parallel", "arbitrary"),
        name="proj_d",
    )(*args)


def _diff_kernel(*refs, lam_init, hb, nqt, tq, seq, past):
    if past:
        q_ref, k_ref, v_ref, kc_ref, vc_ref, lam_ref, g_ref, o_ref, *scr = refs
    else:
        q_ref, k_ref, v_ref, lam_ref, g_ref, o_ref, *scr = refs
        kc_ref = vc_ref = None
    s_refs, e_refs = (scr[0:2], scr[2:4]), scr[4:6]
    lq = lam_ref[...]
    lam = (jnp.exp(jnp.sum(lq[0:1] * lq[1:2], axis=-1, keepdims=True))
           - jnp.exp(jnp.sum(lq[2:3] * lq[3:4], axis=-1, keepdims=True)) + lam_init)
    hw = 2 * DIFF_DH
    lane = lax.broadcasted_iota(jnp.int32, (tq, hw), 1)
    chunks = _key_chunks(seq, past)
    units = [(hd, qt) for hd in range(hb) for qt in range(nqt)]
    col_max = [None] * len(units)
    col_sum = [None] * len(units)

    def scores(u):
        hd, qt = units[u]
        cols = slice(hd * hw, (hd + 1) * hw)
        q = q_ref[qt * tq:(qt + 1) * tq, cols]
        zero = jnp.zeros_like(q)
        halves = (jnp.where(lane < DIFF_DH, q, zero), jnp.where(lane >= DIFF_DH, q, zero))
        m = [None, None]
        for is_cache, off, n, row in chunks:
            kr = kc_ref if is_cache else k_ref
            k = kr[off:off + n, cols]
            for c in range(2):
                s = _dot_t(k, halves[c])
                s_refs[u % 2][c][row:row + n, :] = s
                mc = jnp.max(s, axis=0, keepdims=True)
                m[c] = mc if m[c] is None else jnp.maximum(m[c], mc)
            yield
        col_max[u] = m

    def exponentials(u):
        m = col_max[u]
        l = [None, None]
        for _, _, n, row in chunks:
            for c in range(2):
                e = jnp.exp2(s_refs[u % 2][c][row:row + n, :] - m[c])
                e_refs[c][row:row + n, :] = e
                lc = jnp.sum(e, axis=0, keepdims=True)
                l[c] = lc if l[c] is None else l[c] + lc
            yield
        col_sum[u] = l

    def combine_pv(u):
        hd, qt = units[u]
        cols = slice(hd * hw, (hd + 1) * hw)
        l0, l1 = col_sum[u]
        ratio = lam * l0 / l1
        acc = None
        for is_cache, off, n, row in chunks:
            vr = vc_ref if is_cache else v_ref
            a = (e_refs[0][row:row + n, :] - ratio * e_refs[1][row:row + n, :]).astype(_BF)
            oc = _dot(vr[cols, off:off + n], a)
            acc = oc if acc is None else acc + oc
            yield
        o = (acc / l0).T
        o_ref[qt * tq:(qt + 1) * tq, cols] = (_rms(o, g_ref[...]) * (1.0 - lam_init)).astype(_BF)

    _interleave(scores(0))
    for u in range(len(units)):
        if u + 1 < len(units):
            _interleave(scores(u + 1), exponentials(u))
        else:
            _interleave(exponentials(u))
        _interleave(combine_pv(u))


def _diff_call(qk, vt, kc, vct, lam_qk, g_sub, batch, seq, lam_init):
    t = qk.shape[0]
    heads = DIFF_HEADS
    hw = 2 * DIFF_DH
    hb, nqt, tq = _attn_units(seq, heads, DIFF_Q_TILE, DIFF_UNITS)
    rows = nqt * tq
    nq = seq // rows
    nh = heads // hb
    past = 0 if kc is None else kc.shape[0] // batch
    in_specs = [pl.BlockSpec((rows, hb * hw), lambda b, h, i: (b * nq + i, h)),
                pl.BlockSpec((seq, hb * hw), lambda b, h, i: (b, nh + h)),
                pl.BlockSpec((hb * hw, seq), lambda b, h, i: (h, b))]
    args = [qk, qk, vt]
    if past:
        in_specs += [pl.BlockSpec((past, hb * hw), lambda b, h, i: (b, h)),
                     pl.BlockSpec((hb * hw, past), lambda b, h, i: (h, b))]
        args += [kc, vct]
    in_specs += [pl.BlockSpec(lam_qk.shape, lambda b, h, i: (0, 0)),
                 pl.BlockSpec((1, hw), lambda b, h, i: (0, 0))]
    args += [lam_qk, g_sub]
    return pl.pallas_call(
        functools.partial(_diff_kernel, lam_init=lam_init, hb=hb, nqt=nqt, tq=tq, seq=seq, past=past),
        grid=(batch, nh, nq),
        in_specs=in_specs,
        out_specs=pl.BlockSpec((rows, hb * hw), lambda b, h, i: (b * nq + i, h)),
        out_shape=jax.ShapeDtypeStruct((t, heads * hw), _BF),
        scratch_shapes=[pltpu.VMEM((seq + past, tq), _F32)] * 6,
        compiler_params=_params("parallel", "parallel", "parallel"),
        name="diff_attn",
    )(*args)


POOL_PAD = 16


def _pool_kernel(pz_ref, w_ref, sc_ref, o_ref, pad_ref, *, seq):
    gc = POOL_GC
    t = lax.broadcasted_iota(jnp.int32, (seq, gc), 0)
    zeros = jnp.zeros((POOL_PAD, gc), _F32)
    pad_ref[0:POOL_PAD, :] = zeros
    pad_ref[POOL_PAD + seq:2 * POOL_PAD + seq, :] = zeros
    for g, w in enumerate(POOL_WINDOWS):
        x = pz_ref[:, g * gc:(g + 1) * gc]
        pad_ref[POOL_PAD:POOL_PAD + seq, :] = x
        lo, hi = w // 2, w - w // 2
        tot = pad_ref[POOL_PAD - lo:POOL_PAD - lo + seq, :]
        for dlt in range(-lo + 1, hi):
            tot = tot + pad_ref[POOL_PAD + dlt:POOL_PAD + dlt + seq, :]
        cnt = (jnp.minimum(t + hi, seq) - jnp.maximum(t - lo, 0)).astype(_F32)
        pooled = (tot / cnt - x).astype(_BF)
        y = _dot(pooled, w_ref[g]) * sc_ref[:, g * gc:(g + 1) * gc]
        o_ref[:, g * gc:(g + 1) * gc] = y.astype(_BF)


def _pool_call(pz, w_pool, pool_scale, batch, seq):
    t, pw = pz.shape
    assert max(POOL_WINDOWS) <= POOL_PAD
    return pl.pallas_call(
        functools.partial(_pool_kernel, seq=seq),
        grid=(batch,),
        in_specs=[pl.BlockSpec((seq, pw), lambda b: (b, 0)),
                  pl.BlockSpec(w_pool.shape, lambda b: (0, 0, 0)),
                  pl.BlockSpec((1, pw), lambda b: (0, 0))],
        out_specs=pl.BlockSpec((seq, pw), lambda b: (b, 0)),
        out_shape=jax.ShapeDtypeStruct((t, pw), _BF),
        scratch_shapes=[pltpu.VMEM((seq + 2 * POOL_PAD, POOL_GC), _F32)],
        compiler_params=_params("parallel"),
        name="pool",
    )(pz, w_pool, pool_scale)


def _rope_tables(seq, reps, pad):
    pos = jnp.arange(seq)
    half = MLA_ROPE // 4
    inv = ROPE_BASE ** (-jnp.arange(half, dtype=_F32) / half)
    ang_r = (pos // GRID_W).astype(_F32)[:, None] * inv
    ang_c = (pos % GRID_W).astype(_F32)[:, None] * inv
    cos = jnp.concatenate([jnp.cos(ang_r)] * 2 + [jnp.cos(ang_c)] * 2, axis=-1)
    sin_r, sin_c = jnp.sin(ang_r), jnp.sin(ang_c)
    z = jnp.zeros_like(sin_r)
    sa = jnp.concatenate([-sin_r, z, -sin_c, z], axis=-1)
    sb = jnp.concatenate([z, sin_r, z, sin_c], axis=-1)
    cos, sa, sb = (jnp.tile(a, (1, reps)) for a in (cos, sa, sb))
    if pad:
        cos = jnp.concatenate([cos, jnp.ones((seq, pad), _F32)], axis=-1)
        sa = jnp.concatenate([sa, jnp.zeros((seq, pad), _F32)], axis=-1)
        sb = jnp.concatenate([sb, jnp.zeros((seq, pad), _F32)], axis=-1)
    return cos, sa, sb


def _prep_weights(w_in_a, w_q_up, w_kv_up, w_in_d):
    na, d, _ = w_in_a.shape
    ql, kl, fw = Q_LORA, KV_LORA, FNET_GROUPS * FNET_GC
    q, ckv, kr, fz = (w_in_a[..., :ql], w_in_a[..., ql:ql + kl], w_in_a[..., ql + kl:ql + kl + MLA_ROPE],
                      w_in_a[..., ql + kl + MLA_ROPE:])
    w_in = jnp.concatenate([q, ckv, fz, kr, jnp.zeros((na, d, LANES - MLA_ROPE), _F32)], axis=-1).astype(_BF)
    scale = LOG2_E / math.sqrt(MLA_NOPE + MLA_ROPE)
    wq = (w_q_up * scale).reshape(na, ql, MLA_HEADS, MLA_NOPE + MLA_ROPE)
    wq = jnp.pad(wq, ((0, 0), (0, 0), (0, 0), (0, LANES - MLA_ROPE))).reshape(na, ql, -1).astype(_BF)
    wkv = w_kv_up.astype(_BF)
    pw, qk = len(POOL_WINDOWS) * POOL_GC, DIFF_HEADS * 2 * DIFF_DH
    wd = jnp.concatenate([w_in_d[..., :pw], w_in_d[..., pw:pw + qk] * (LOG2_E / math.sqrt(DIFF_DH)),
                          w_in_d[..., pw + qk:]], axis=-1).astype(_BF)
    return w_in, wq, wkv, wd


def _trunk(x, mod, batch, seq, caches, p, tpb_of):
    t, d = x.shape
    depth = p["g_norm"].shape[0]
    decode = caches is not None
    tm = min(TOKEN_TILE, seq) if decode else min(TOKEN_TILE, t)
    tpb = seq // tm if decode else None
    states = ([], [], [], [])
    if decode:
        tab_mla = _rope_tables(seq, 1, LANES - MLA_ROPE)
        tab_diff = _rope_tables(seq, LANES // DIFF_DH, 0)
    else:
        tab_mla = tab_diff = None
    dft_s = _dft_tables(seq)
    dft_c = _dft_tables(FNET_GC)
    aw = MLA_HEADS * MLA_V
    pw = len(POOL_WINDOWS) * POOL_GC
    qk = DIFF_HEADS * 2 * DIFF_DH
    for l in range(depth):
        i = l // 2
        g = p["g_norm"][l]
        x = _ffn_call(x, mod, g[0:1], p["wg"], p["wu"], p["wd"], l, 0, tm, tpb)
        if l % 2 == 0:
            outs = _proj_a_call(x, mod, g[1:2], p["w_in_a"][i], p["g_q"][i:i + 1], p["g_kv"][i:i + 1], p["wq"][i],
                                tab_mla, l, tm, tpb, not decode)
            q, ckv, kr, fz = outs[:4]
            if not decode:
                states[0].append(outs[4])
                states[1].append(outs[5][:, :MLA_ROPE])
            k, v = _kvup_call(ckv, kr, p["wkv"][i], tm)
            kc = vc = None
            if decode:
                past = caches[0].shape[2]
                c_ckv = caches[0][:, i].reshape(batch * past, KV_LORA).astype(_BF)
                c_kr = jnp.pad(caches[1][:, i].reshape(batch * past, MLA_ROPE), ((0, 0), (0, LANES - MLA_ROPE))).astype(_BF)
                kc, vc = _kvup_call(c_ckv, c_kr, p["wkv"][i], min(TOKEN_TILE, past))
            attn = _mla_call(q, k, v, kc, vc, batch, seq)
            four = _fourier_call(fz, dft_s, dft_c, p["w_fnet"][i], batch, seq)
            x = _oproj_call(attn, four, p["w_o_a"][i, :aw], p["w_o_a"][i, aw:], x, mod, l, tm, tpb)
        else:
            lam_init = 0.8 - 0.6 * math.exp(-0.3 * l)
            tm_d = min(PROJ_D_TILE, seq) if decode else tm
            outs = _proj_d_call(x, mod, g[1:2], p["w_in_d"][i], tab_diff, l, tm_d, seq // tm_d if decode else None,
                                not decode)
            pz, qk_new, vt_new = outs[:3]
            if not decode:
                states[2].append(outs[3][:, :qk])
                states[3].append(outs[3][:, qk:])
            kc = vct = None
            if decode:
                past = caches[2].shape[2]
                kc = caches[2][:, i].reshape(batch * past, qk).astype(_BF)
                vct = caches[3][:, i].reshape(batch * past, qk).T.astype(_BF)
            o = _diff_call(qk_new, vt_new, kc, vct, p["lam_qk"][i], p["g_sub"][i:i + 1], batch, seq, lam_init)
            pool = _pool_call(pz, p["w_pool"][i], p["pool_scale"][i:i + 1], batch, seq)
            x = _oproj_call(pool, o, p["w_o_d"][i, :pw], p["w_o_d"][i, pw:], x, mod, l, tm, tpb)
        x = _ffn_call(x, mod, g[2:3], p["wg"], p["wu"], p["wd"], l, 1, tm, tpb,
                      g_final=p["g_final"] if l == depth - 1 else None)
    return x, states


def kernel(x_prompt, x_sample, cache_mla_ckv, cache_mla_krope, cache_diff_k, cache_diff_v, c, c_ctx, w_mod, b_mod, g_norm, w_ffn_gate, w_ffn_up, w_ffn_down, w_in_a, g_q, g_kv, w_q_up, w_kv_up, w_fnet, w_o_a, w_in_d, lam_qk, g_sub, w_pool, pool_scale, w_o_d, g_final):
    batch, seq, d = x_prompt.shape
    dbatch, dseq, _ = x_sample.shape
    assert 1 + dbatch <= MOD_ROWS
    cond = jnp.concatenate([c_ctx[None, :], c, jnp.zeros((MOD_ROWS - 1 - dbatch, d), _F32)], axis=0)
    mod = _mod_call(cond, w_mod, b_mod)

    w_in, wq, wkv, wd_in = _prep_weights(w_in_a, w_q_up, w_kv_up, w_in_d)
    p = dict(g_norm=g_norm, wg=w_ffn_gate.astype(_BF), wu=w_ffn_up.astype(_BF), wd=w_ffn_down.astype(_BF),
             w_in_a=w_in, g_q=g_q, g_kv=g_kv, wq=wq, wkv=wkv, w_fnet=w_fnet.astype(_BF), w_o_a=w_o_a.astype(_BF),
             w_in_d=wd_in, lam_qk=lam_qk, g_sub=g_sub, w_pool=w_pool.astype(_BF), pool_scale=pool_scale,
             w_o_d=w_o_d.astype(_BF), g_final=g_final[None, :])

    y_p, st = _trunk(x_prompt.reshape(batch * seq, d), mod, batch, seq, None, p, None)
    caches = (cache_mla_ckv, cache_mla_krope, cache_diff_k, cache_diff_v)
    y_s, _ = _trunk(x_sample.reshape(dbatch * dseq, d), mod, dbatch, dseq, caches, p, None)

    new_ckv = jnp.stack([s.reshape(batch, seq, KV_LORA) for s in st[0]], axis=1)
    new_kr = jnp.stack([s.reshape(batch, seq, MLA_ROPE) for s in st[1]], axis=1)
    new_k = jnp.stack([s.reshape(batch, seq, DIFF_HEADS, 2, DIFF_DH) for s in st[2]], axis=1)
    new_v = jnp.stack([s.reshape(batch, seq, DIFF_HEADS, 2 * DIFF_DH) for s in st[3]], axis=1)
    return (y_p.reshape(batch, seq, d), y_s.reshape(dbatch, dseq, d), new_ckv, new_kr, new_k, new_v)
```

```python
import functools
import math

import jax
import jax.numpy as jnp
from jax import lax
from jax.experimental import pallas as pl
from jax.experimental.pallas import tpu as pltpu

GRID_W = 64
MLA_HEADS = 12
MLA_NOPE = 128
MLA_ROPE = 64
MLA_V = 128
Q_LORA = 512
KV_LORA = 512
FNET_GROUPS = 4
FNET_GC = 128
POOL_WINDOWS = (2, 4, 8, 16)
POOL_GC = 128
DIFF_HEADS = 12
DIFF_DH = 64
N_MOD = 9
ROPE_BASE = 10000.0
EPS = 1e-6
LOG2_E = math.log2(math.e)

LANES = 128
MXU_DEPTH = 256
VMEM_BYTES = 64 * 2 ** 20
VMEM_LIMIT = VMEM_BYTES - 8 * 2 ** 20

TOKEN_TILE = 512
FFN_TILE = 512
PROJ_D_ROW_CHUNK = 256
PROJ_D_TILE = 1024
MLA_Q_TILE = 256
MLA_UNITS = 8
DIFF_Q_TILE = 256
DIFF_UNITS = 8
ATTN_KEY_CHUNK = 512
MOD_ROWS = 16

_BF = jnp.bfloat16
_F32 = jnp.float32


def _tile(n, pref):
    if n <= pref:
        return n
    t = (pref // LANES) * LANES
    while n % t:
        t -= LANES
    return t


def _params(*sem):
    return pltpu.CompilerParams(dimension_semantics=sem, vmem_limit_bytes=VMEM_LIMIT)


def _dot(a, b):
    return jnp.dot(a, b, preferred_element_type=_F32)


def _dot_t(a, b):
    return lax.dot_general(a, b, (((1,), (1,)), ((), ())), preferred_element_type=_F32)


def _rms(x, g):
    ms = jnp.mean(x * x, axis=-1, keepdims=True)
    return x * lax.rsqrt(ms + EPS) * g


def _norm_mod(x, g, shift, scale):
    return _rms(x, g) * (1.0 + scale) + shift


def _silu(x):
    return x / (1.0 + jnp.exp(-x))


def _rope(x, c, sa, sb):
    return x * c + pltpu.roll(x, LANES - 16, 1) * sa + pltpu.roll(x, 16, 1) * sb


def _mod_row(layer, tiles_per_batch):
    if tiles_per_batch is None:
        return lambda i: layer * MOD_ROWS
    return lambda i: layer * MOD_ROWS + 1 + i // tiles_per_batch


def _mod_kernel(c_ref, w_ref, b_ref, o_ref):
    s = _silu(c_ref[...]).astype(_BF)
    o_ref[...] = _dot(s, w_ref[...].astype(_BF)) + b_ref[...]


def _mod_call(cond, w_mod, b_mod):
    depth, d, n = w_mod.shape
    r = cond.shape[0]
    tn = _tile(n, 1024)
    out = pl.pallas_call(
        _mod_kernel,
        grid=(depth, n // tn),
        in_specs=[pl.BlockSpec((r, d), lambda l, j: (0, 0)),
                  pl.BlockSpec((None, d, tn), lambda l, j: (l, 0, j)),
                  pl.BlockSpec((None, 1, tn), lambda l, j: (l, 0, j))],
        out_specs=pl.BlockSpec((None, r, tn), lambda l, j: (l, 0, j)),
        out_shape=jax.ShapeDtypeStruct((depth, r, n), _F32),
        compiler_params=_params("parallel", "parallel"),
        name="mod",
    )(cond, w_mod, b_mod.reshape(depth, 1, n))
    return out.reshape(depth * r, N_MOD, d)


def _ffn_kernel(x_ref, m_ref, g_ref, wg_ref, wu_ref, wd_ref, gf_ref, o_ref, h_ref, acc_ref, *, base, final):
    j = pl.program_id(1)

    @pl.when(j == 0)
    def _():
        h = _norm_mod(x_ref[...], g_ref[...], m_ref[base:base + 1, :], m_ref[base + 1:base + 2, :])
        h_ref[...] = h.astype(_BF)
        acc_ref[...] = jnp.zeros_like(acc_ref)

    h = h_ref[...]
    a = _silu(_dot(h, wg_ref[...])) * _dot(h, wu_ref[...])
    acc_ref[...] += _dot(a.astype(_BF), wd_ref[...])

    @pl.when(j == pl.num_programs(1) - 1)
    def _():
        y = x_ref[...] + (0.5 * m_ref[base + 2:base + 3, :]) * acc_ref[...]
        if final:
            y = _rms(y, gf_ref[...])
        o_ref[...] = y


def _ffn_call(x, mod, g, wg, wu, wd, layer, sub, tm, tpb, g_final=None):
    t, d = x.shape
    f = wg.shape[-1]
    tf = _tile(f, FFN_TILE)
    row = _mod_row(layer, tpb)
    final = g_final is not None
    gf = g_final if final else g
    return pl.pallas_call(
        functools.partial(_ffn_kernel, base=0 if sub == 0 else 6, final=final),
        grid=(t // tm, f // tf),
        in_specs=[pl.BlockSpec((tm, d), lambda i, j: (i, 0)),
                  pl.BlockSpec((None, N_MOD, d), lambda i, j: (row(i), 0, 0)),
                  pl.BlockSpec((1, d), lambda i, j: (0, 0)),
                  pl.BlockSpec((None, None, d, tf), lambda i, j: (layer, sub, 0, j)),
                  pl.BlockSpec((None, None, d, tf), lambda i, j: (layer, sub, 0, j)),
                  pl.BlockSpec((None, None, tf, d), lambda i, j: (layer, sub, j, 0)),
                  pl.BlockSpec((1, d), lambda i, j: (0, 0))],
        out_specs=pl.BlockSpec((tm, d), lambda i, j: (i, 0)),
        out_shape=jax.ShapeDtypeStruct((t, d), _F32),
        scratch_shapes=[pltpu.VMEM((tm, d), _BF), pltpu.VMEM((tm, d), _F32)],
        compiler_params=_params("parallel", "arbitrary"),
        name="ffn",
    )(x, mod, g, wg, wu, wd, gf)


def _oproj_kernel(a1_ref, a2_ref, w1_ref, w2_ref, x_ref, m_ref, o_ref):
    acc = _dot(a1_ref[...], w1_ref[...]) + _dot(a2_ref[...], w2_ref[...])
    o_ref[...] = x_ref[...] + m_ref[5:6, :] * acc


def _oproj_call(a1, a2, w1, w2, x, mod, layer, tm, tpb):
    t, d = x.shape
    k1, k2 = a1.shape[1], a2.shape[1]
    row = _mod_row(layer, tpb)
    return pl.pallas_call(
        _oproj_kernel,
        grid=(t // tm,),
        in_specs=[pl.BlockSpec((tm, k1), lambda i: (i, 0)),
                  pl.BlockSpec((tm, k2), lambda i: (i, 0)),
                  pl.BlockSpec((k1, d), lambda i: (0, 0)),
                  pl.BlockSpec((k2, d), lambda i: (0, 0)),
                  pl.BlockSpec((tm, d), lambda i: (i, 0)),
                  pl.BlockSpec((None, N_MOD, d), lambda i: (row(i), 0, 0))],
        out_specs=pl.BlockSpec((tm, d), lambda i: (i, 0)),
        out_shape=jax.ShapeDtypeStruct((t, d), _F32),
        compiler_params=_params("parallel"),
        name="oproj",
    )(a1, a2, w1, w2, x, mod)


def _proj_a_kernel(*refs, rope, state, heads):
    x_ref, m_ref, g_ref, win_ref, gq_ref, gkv_ref, wq_ref = refs[:7]
    refs = refs[7:]
    if rope:
        c_ref, sa_ref, sb_ref = refs[:3]
        refs = refs[3:]
    q_ref, ckv_ref, kr_ref, fz_ref = refs[:4]
    h = _norm_mod(x_ref[...], g_ref[...], m_ref[3:4, :], m_ref[4:5, :]).astype(_BF)
    u = _dot(h, win_ref[...])
    ql, kl = Q_LORA, KV_LORA
    fw = FNET_GROUPS * FNET_GC
    qn = _rms(u[:, :ql], gq_ref[...]).astype(_BF)
    ckv = _rms(u[:, ql:ql + kl], gkv_ref[...])
    kr = u[:, ql + kl + fw:]
    ckv_ref[...] = ckv.astype(_BF)
    fz_ref[...] = u[:, ql + kl:ql + kl + fw].astype(_BF)
    if state:
        refs[4][...] = ckv
        refs[5][...] = kr
    q = _dot(qn, wq_ref[...])
    hw = MLA_NOPE + LANES
    if rope:
        c, sa, sb = c_ref[...], sa_ref[...], sb_ref[...]
        kr = _rope(kr, c, sa, sb)
        for hd in range(heads):
            q_ref[:, hd * hw:hd * hw + MLA_NOPE] = q[:, hd * hw:hd * hw + MLA_NOPE].astype(_BF)
            q_ref[:, hd * hw + MLA_NOPE:(hd + 1) * hw] = _rope(q[:, hd * hw + MLA_NOPE:(hd + 1) * hw], c, sa, sb).astype(_BF)
    else:
        q_ref[...] = q.astype(_BF)
    kr_ref[...] = kr.astype(_BF)


def _proj_a_call(x, mod, g, w_in, g_q, g_kv, wq, tables, layer, tm, tpb, state):
    t, d = x.shape
    n_in = w_in.shape[1]
    nq = wq.shape[1]
    fw = FNET_GROUPS * FNET_GC
    rope = tables is not None
    row = _mod_row(layer, tpb)
    in_specs = [pl.BlockSpec((tm, d), lambda i: (i, 0)),
                pl.BlockSpec((None, N_MOD, d), lambda i: (row(i), 0, 0)),
                pl.BlockSpec((1, d), lambda i: (0, 0)),
                pl.BlockSpec((d, n_in), lambda i: (0, 0)),
                pl.BlockSpec((1, Q_LORA), lambda i: (0, 0)),
                pl.BlockSpec((1, KV_LORA), lambda i: (0, 0)),
                pl.BlockSpec((Q_LORA, nq), lambda i: (0, 0))]
    args = [x, mod, g, w_in, g_q, g_kv, wq]
    if rope:
        in_specs += [pl.BlockSpec((tm, LANES), lambda i: (i % tpb, 0))] * 3
        args += list(tables)
    shapes = [((t, nq), _BF), ((t, KV_LORA), _BF), ((t, LANES), _BF), ((t, fw), _BF)]
    if state:
        shapes += [((t, KV_LORA), _F32), ((t, LANES), _F32)]
    return pl.pallas_call(
        functools.partial(_proj_a_kernel, rope=rope, state=state, heads=MLA_HEADS),
        grid=(t // tm,),
        in_specs=in_specs,
        out_specs=[pl.BlockSpec((tm, s[1]), lambda i: (i, 0)) for s, _ in shapes],
        out_shape=[jax.ShapeDtypeStruct(s, dt) for s, dt in shapes],
        compiler_params=_params("parallel"),
        name="proj_a",
    )(*args)


def _kvup_kernel(ckv_ref, kr_ref, w_ref, k_ref, v_ref, *, heads):
    kv = _dot(ckv_ref[...], w_ref[...])
    kr = kr_ref[...]
    hw = MLA_NOPE + MLA_V
    for hd in range(heads):
        k_ref[hd, :, :MLA_NOPE] = kv[:, hd * hw:hd * hw + MLA_NOPE].astype(_BF)
        k_ref[hd, :, MLA_NOPE:] = kr
        v_ref[hd] = kv[:, hd * hw + MLA_NOPE:(hd + 1) * hw].T.astype(_BF)


def _kvup_call(ckv, kr, wkv, tm):
    t = ckv.shape[0]
    heads = MLA_HEADS
    return pl.pallas_call(
        functools.partial(_kvup_kernel, heads=heads),
        grid=(t // tm,),
        in_specs=[pl.BlockSpec((tm, KV_LORA), lambda i: (i, 0)),
                  pl.BlockSpec((tm, LANES), lambda i: (i, 0)),
                  pl.BlockSpec(wkv.shape, lambda i: (0, 0))],
        out_specs=[pl.BlockSpec((heads, tm, MLA_NOPE + LANES), lambda i: (0, i, 0)),
                   pl.BlockSpec((heads, MLA_V, tm), lambda i: (0, 0, i))],
        out_shape=[jax.ShapeDtypeStruct((heads, t, MLA_NOPE + LANES), _BF),
                   jax.ShapeDtypeStruct((heads, MLA_V, t), _BF)],
        compiler_params=_params("parallel"),
        name="kvup",
    )(ckv, kr, wkv)


def _interleave(*stages):
    live = list(stages)
    while live:
        for st in list(live):
            try:
                next(st)
            except StopIteration:
                live.remove(st)


def _key_chunks(seq, past):
    kc = min(ATTN_KEY_CHUNK, seq)
    chunks = [(False, c * kc, kc, c * kc) for c in range(seq // kc)]
    if past:
        pc = min(ATTN_KEY_CHUNK, past)
        chunks += [(True, c * pc, pc, seq + c * pc) for c in range(past // pc)]
    return chunks


def _attn_units(seq, heads, q_tile, n_units):
    tq = min(q_tile, seq)
    nqt = min(n_units, seq // tq)
    hb = math.gcd(heads, max(n_units // nqt, 1))
    return hb, nqt, tq


def _mla_kernel(*refs, hb, nqt, tq, seq, past):
    if past:
        q_ref, k_ref, v_ref, kc_ref, vc_ref, o_ref, *s_refs = refs
    else:
        q_ref, k_ref, v_ref, o_ref, *s_refs = refs
        kc_ref = vc_ref = None
    qw = MLA_NOPE + LANES
    chunks = _key_chunks(seq, past)
    units = [(hd, qt) for hd in range(hb) for qt in range(nqt)]
    col_max = [None] * len(units)

    def scores(u):
        hd, qt = units[u]
        q = q_ref[qt * tq:(qt + 1) * tq, hd * qw:(hd + 1) * qw]
        m = None
        for is_cache, off, n, row in chunks:
            kr = kc_ref if is_cache else k_ref
            s = _dot_t(kr[hd, off:off + n, :], q)
            s_refs[u % 2][row:row + n, :] = s
            mc = jnp.max(s, axis=0, keepdims=True)
            m = mc if m is None else jnp.maximum(m, mc)
            yield
        col_max[u] = m

    def softmax_pv(u):
        hd, qt = units[u]
        m = col_max[u]
        l = acc = None
        for is_cache, off, n, row in chunks:
            vr = vc_ref if is_cache else v_ref
            p = jnp.exp2(s_refs[u % 2][row:row + n, :] - m)
            lc = jnp.sum(p, axis=0, keepdims=True)
            oc = _dot(vr[hd, :, off:off + n], p.astype(_BF))
            l = lc if l is None else l + lc
            acc = oc if acc is None else acc + oc
            yield
        o_ref[qt * tq:(qt + 1) * tq, hd * MLA_V:(hd + 1) * MLA_V] = (acc / l).T.astype(_BF)

    _interleave(scores(0))
    for u in range(len(units)):
        if u + 1 < len(units):
            _interleave(scores(u + 1), softmax_pv(u))
        else:
            _interleave(softmax_pv(u))


def _mla_call(q, k, v, kc, vc, batch, seq):
    t = q.shape[0]
    heads = MLA_HEADS
    hb, nqt, tq = _attn_units(seq, heads, MLA_Q_TILE, MLA_UNITS)
    rows = nqt * tq
    nq = seq // rows
    qw = MLA_NOPE + LANES
    past = 0 if kc is None else kc.shape[1] // batch
    in_specs = [pl.BlockSpec((rows, hb * qw), lambda b, h, i: (b * nq + i, h)),
                pl.BlockSpec((hb, seq, qw), lambda b, h, i: (h, b, 0)),
                pl.BlockSpec((hb, MLA_V, seq), lambda b, h, i: (h, 0, b))]
    args = [q, k, v]
    if past:
        in_specs += [pl.BlockSpec((hb, past, qw), lambda b, h, i: (h, b, 0)),
                     pl.BlockSpec((hb, MLA_V, past), lambda b, h, i: (h, 0, b))]
        args += [kc, vc]
    return pl.pallas_call(
        functools.partial(_mla_kernel, hb=hb, nqt=nqt, tq=tq, seq=seq, past=past),
        grid=(batch, heads // hb, nq),
        in_specs=in_specs,
        out_specs=pl.BlockSpec((rows, hb * MLA_V), lambda b, h, i: (b * nq + i, h)),
        out_shape=jax.ShapeDtypeStruct((t, heads * MLA_V), _BF),
        scratch_shapes=[pltpu.VMEM((seq + past, tq), _F32)] * 2,
        compiler_params=_params("parallel", "parallel", "parallel"),
        name="mla_attn",
    )(*args)


def _fourier_kernel(cs_ref, sn_ref, z_ref, cc_ref, sc_ref, w_ref, o_ref, *, norm):
    z = z_ref[...]
    a = _dot(cs_ref[...], z)
    b = _dot(sn_ref[...], z)
    gc = FNET_GC
    for g in range(FNET_GROUPS):
        f = _dot(a[:, g * gc:(g + 1) * gc].astype(_BF), cc_ref[...]) - _dot(b[:, g * gc:(g + 1) * gc].astype(_BF), sc_ref[...])
        f = (f * norm).astype(_BF)
        o_ref[:, g * gc:(g + 1) * gc] = _dot(f, w_ref[g]).astype(_BF)


def _dft_tables(n):
    k = lax.broadcasted_iota(jnp.int32, (n, n), 0) * lax.broadcasted_iota(jnp.int32, (n, n), 1) % n
    ang = k.astype(_F32) * (2.0 * math.pi / n)
    return jnp.cos(ang).astype(_BF), jnp.sin(ang).astype(_BF)


def _fourier_call(z, tabs_s, tabs_c, w, batch, seq):
    t, fw = z.shape
    tq = min(TOKEN_TILE, seq)
    nq = seq // tq
    cs, sn = tabs_s
    cc, sc = tabs_c
    gc = FNET_GC
    return pl.pallas_call(
        functools.partial(_fourier_kernel, norm=1.0 / math.sqrt(seq * gc)),
        grid=(nq, batch),
        in_specs=[pl.BlockSpec((tq, seq), lambda i, b: (i, 0)),
                  pl.BlockSpec((tq, seq), lambda i, b: (i, 0)),
                  pl.BlockSpec((seq, fw), lambda i, b: (b, 0)),
                  pl.BlockSpec((gc, gc), lambda i, b: (0, 0)),
                  pl.BlockSpec((gc, gc), lambda i, b: (0, 0)),
                  pl.BlockSpec(w.shape, lambda i, b: (0, 0, 0))],
        out_specs=pl.BlockSpec((tq, fw), lambda i, b: (b * nq + i, 0)),
        out_shape=jax.ShapeDtypeStruct((t, fw), _BF),
        compiler_params=_params("parallel", "parallel"),
        name="fourier",
    )(cs, sn, z, cc, sc, w)


def _proj_d_kernel(*refs, rope, state, npz, nqk, chunks):
    x_ref, m_ref, g_ref, w_ref = refs[:4]
    refs = refs[4:]
    if rope:
        c_ref, sa_ref, sb_ref = refs[:3]
        refs = refs[3:]
    pz_ref, qk_ref, vt_ref = refs[:3]
    h_ref = refs[-1]
    j = pl.program_id(1)

    @pl.when(j == 0)
    def _():
        h_ref[...] = _norm_mod(x_ref[...], g_ref[...], m_ref[3:4, :], m_ref[4:5, :]).astype(_BF)

    tm = h_ref.shape[0]
    rc = min(PROJ_D_ROW_CHUNK, tm)
    row_chunks = [slice(r, r + rc) for r in range(0, tm, rc)]

    @pl.when(j < npz)
    def _():
        for rows in row_chunks:
            pz_ref[rows, :] = _dot(h_ref[rows, :], w_ref[...])

    def rotary_tiles(keep):
        for rows in row_chunks:
            acc = _dot(h_ref[rows, :], w_ref[...])
            if keep:
                refs[3][rows, :] = acc
            if rope:
                c, sa, sb = c_ref[rows, :], sa_ref[rows, :], sb_ref[rows, :]
                for k in range(chunks):
                    cols = slice(k * LANES, (k + 1) * LANES)
                    qk_ref[rows, cols] = _rope(acc[:, cols], c, sa, sb).astype(_BF)
            else:
                qk_ref[rows, :] = acc.astype(_BF)

    @pl.when((j >= npz) & (j < npz + nqk))
    def _():
        rotary_tiles(False)

    @pl.when((j >= npz + nqk) & (j < npz + 2 * nqk))
    def _():
        rotary_tiles(state)

    @pl.when(j >= npz + 2 * nqk)
    def _():
        for rows in row_chunks:
            acc = _dot(h_ref[rows, :], w_ref[...])
            if state:
                refs[3][rows, :] = acc
            vt_ref[:, rows] = acc.T.astype(_BF)


def _proj_d_call(x, mod, g, w_in, tables, layer, tm, tpb, state):
    t, d = x.shape
    n = w_in.shape[1]
    pw = len(POOL_WINDOWS) * POOL_GC
    qk = DIFF_HEADS * 2 * DIFF_DH
    tn = math.gcd(math.gcd(pw, qk), 512)
    npz, nqk = pw // tn, qk // tn
    rope = tables is not None
    row = _mod_row(layer, tpb)
    in_specs = [pl.BlockSpec((tm, d), lambda i, j: (i, 0)),
                pl.BlockSpec((None, N_MOD, d), lambda i, j: (row(i), 0, 0)),
                pl.BlockSpec((1, d), lambda i, j: (0, 0)),
                pl.BlockSpec((d, tn), lambda i, j: (0, j))]
    args = [x, mod, g, w_in]
    if rope:
        in_specs += [pl.BlockSpec((tm, LANES), lambda i, j: (i % tpb, 0))] * 3
        args += list(tables)
    out_specs = [pl.BlockSpec((tm, tn), lambda i, j: (i, jnp.minimum(j, npz - 1))),
                 pl.BlockSpec((tm, tn), lambda i, j: (i, jnp.clip(j - npz, 0, 2 * nqk - 1))),
                 pl.BlockSpec((tn, tm), lambda i, j: (jnp.maximum(j - npz - 2 * nqk, 0), i))]
    out_shape = [jax.ShapeDtypeStruct((t, pw), _F32), jax.ShapeDtypeStruct((t, 2 * qk), _BF),
                 jax.ShapeDtypeStruct((qk, t), _BF)]
    if state:
        out_specs.append(pl.BlockSpec((tm, tn), lambda i, j: (i, jnp.maximum(j - npz - nqk, 0))))
        out_shape.append(jax.ShapeDtypeStruct((t, 2 * qk), _F32))
    return pl.pallas_call(
        functools.partial(_proj_d_kernel, rope=rope, state=state, npz=npz, nqk=nqk, chunks=tn // LANES),
        grid=(t // tm, n // tn),
        in_specs=in_specs,
        out_specs=out_specs,
        out_shape=out_shape,
        scratch_shapes=[pltpu.VMEM((tm, d), _BF)],
        compiler_params=_params("parallel", "arbitrary"),
        name="proj_d",
    )(*args)


def _diff_kernel(*refs, lam_init, hb, nqt, tq, seq, past):
    if past:
        q_ref, k_ref, v_ref, kc_ref, vc_ref, lam_ref, g_ref, o_ref, *scr = refs
    else:
        q_ref, k_ref, v_ref, lam_ref, g_ref, o_ref, *scr = refs
        kc_ref = vc_ref = None
    s_refs, e_refs = (scr[0:2], scr[2:4]), scr[4:6]
    lq = lam_ref[...]
    lam = (jnp.exp(jnp.sum(lq[0:1] * lq[1:2], axis=-1, keepdims=True))
           - jnp.exp(jnp.sum(lq[2:3] * lq[3:4], axis=-1, keepdims=True)) + lam_init)
    hw = 2 * DIFF_DH
    lane = lax.broadcasted_iota(jnp.int32, (tq, hw), 1)
    chunks = _key_chunks(seq, past)
    units = [(hd, qt) for hd in range(hb) for qt in range(nqt)]
    col_max = [None] * len(units)
    col_sum = [None] * len(units)

    def scores(u):
        hd, qt = units[u]
        cols = slice(hd * hw, (hd + 1) * hw)
        q = q_ref[qt * tq:(qt + 1) * tq, cols]
        zero = jnp.zeros_like(q)
        halves = (jnp.where(lane < DIFF_DH, q, zero), jnp.where(lane >= DIFF_DH, q, zero))
        m = [None, None]
        for is_cache, off, n, row in chunks:
            kr = kc_ref if is_cache else k_ref
            k = kr[off:off + n, cols]
            for c in range(2):
                s = _dot_t(k, halves[c])
                s_refs[u % 2][c][row:row + n, :] = s
                mc = jnp.max(s, axis=0, keepdims=True)
                m[c] = mc if m[c] is None else jnp.maximum(m[c], mc)
            yield
        col_max[u] = m

    def exponentials(u):
        m = col_max[u]
        l = [None, None]
        for _, _, n, row in chunks:
            for c in range(2):
                e = jnp.exp2(s_refs[u % 2][c][row:row + n, :] - m[c])
                e_refs[c][row:row + n, :] = e
                lc = jnp.sum(e, axis=0, keepdims=True)
                l[c] = lc if l[c] is None else l[c] + lc
            yield
        col_sum[u] = l

    def combine_pv(u):
        hd, qt = units[u]
        cols = slice(hd * hw, (hd + 1) * hw)
        l0, l1 = col_sum[u]
        ratio = lam * l0 / l1
        acc = None
        for is_cache, off, n, row in chunks:
            vr = vc_ref if is_cache else v_ref
            a = (e_refs[0][row:row + n, :] - ratio * e_refs[1][row:row + n, :]).astype(_BF)
            oc = _dot(vr[cols, off:off + n], a)
            acc = oc if acc is None else acc + oc
            yield
        o = (acc / l0).T
        o_ref[qt * tq:(qt + 1) * tq, cols] = (_rms(o, g_ref[...]) * (1.0 - lam_init)).astype(_BF)

    _interleave(scores(0))
    for u in range(len(units)):
        if u + 1 < len(units):
            _interleave(scores(u + 1), exponentials(u))
        else:
            _interleave(exponentials(u))
        _interleave(combine_pv(u))


def _diff_call(qk, vt, kc, vct, lam_qk, g_sub, batch, seq, lam_init):
    t = qk.shape[0]
    heads = DIFF_HEADS
    hw = 2 * DIFF_DH
    hb, nqt, tq = _attn_units(seq, heads, DIFF_Q_TILE, DIFF_UNITS)
    rows = nqt * tq
    nq = seq // rows
    nh = heads // hb
    past = 0 if kc is None else kc.shape[0] // batch
    in_specs = [pl.BlockSpec((rows, hb * hw), lambda b, h, i: (b * nq + i, h)),
                pl.BlockSpec((seq, hb * hw), lambda b, h, i: (b, nh + h)),
                pl.BlockSpec((hb * hw, seq), lambda b, h, i: (h, b))]
    args = [qk, qk, vt]
    if past:
        in_specs += [pl.BlockSpec((past, hb * hw), lambda b, h, i: (b, h)),
                     pl.BlockSpec((hb * hw, past), lambda b, h, i: (h, b))]
        args += [kc, vct]
    in_specs += [pl.BlockSpec(lam_qk.shape, lambda b, h, i: (0, 0)),
                 pl.BlockSpec((1, hw), lambda b, h, i: (0, 0))]
    args += [lam_qk, g_sub]
    return pl.pallas_call(
        functools.partial(_diff_kernel, lam_init=lam_init, hb=hb, nqt=nqt, tq=tq, seq=seq, past=past),
        grid=(batch, nh, nq),
        in_specs=in_specs,
        out_specs=pl.BlockSpec((rows, hb * hw), lambda b, h, i: (b * nq + i, h)),
        out_shape=jax.ShapeDtypeStruct((t, heads * hw), _BF),
        scratch_shapes=[pltpu.VMEM((seq + past, tq), _F32)] * 6,
        compiler_params=_params("parallel", "parallel", "parallel"),
        name="diff_attn",
    )(*args)


POOL_PAD = 16


def _pool_kernel(pz_ref, w_ref, sc_ref, o_ref, pad_ref, *, seq):
    gc = POOL_GC
    t = lax.broadcasted_iota(jnp.int32, (seq, gc), 0)
    zeros = jnp.zeros((POOL_PAD, gc), _F32)
    pad_ref[0:POOL_PAD, :] = zeros
    pad_ref[POOL_PAD + seq:2 * POOL_PAD + seq, :] = zeros
    for g, w in enumerate(POOL_WINDOWS):
        x = pz_ref[:, g * gc:(g + 1) * gc]
        pad_ref[POOL_PAD:POOL_PAD + seq, :] = x
        lo, hi = w // 2, w - w // 2
        tot = pad_ref[POOL_PAD - lo:POOL_PAD - lo + seq, :]
        for dlt in range(-lo + 1, hi):
            tot = tot + pad_ref[POOL_PAD + dlt:POOL_PAD + dlt + seq, :]
        cnt = (jnp.minimum(t + hi, seq) - jnp.maximum(t - lo, 0)).astype(_F32)
        pooled = (tot / cnt - x).astype(_BF)
        y = _dot(pooled, w_ref[g]) * sc_ref[:, g * gc:(g + 1) * gc]
        o_ref[:, g * gc:(g + 1) * gc] = y.astype(_BF)


def _pool_call(pz, w_pool, pool_scale, batch, seq):
    t, pw = pz.shape
    assert max(POOL_WINDOWS) <= POOL_PAD
    return pl.pallas_call(
        functools.partial(_pool_kernel, seq=seq),
        grid=(batch,),
        in_specs=[pl.BlockSpec((seq, pw), lambda b: (b, 0)),
                  pl.BlockSpec(w_pool.shape, lambda b: (0, 0, 0)),
                  pl.BlockSpec((1, pw), lambda b: (0, 0))],
        out_specs=pl.BlockSpec((seq, pw), lambda b: (b, 0)),
        out_shape=jax.ShapeDtypeStruct((t, pw), _BF),
        scratch_shapes=[pltpu.VMEM((seq + 2 * POOL_PAD, POOL_GC), _F32)],
        compiler_params=_params("parallel"),
        name="pool",
    )(pz, w_pool, pool_scale)


def _rope_tables(seq, reps, pad):
    pos = jnp.arange(seq)
    half = MLA_ROPE // 4
    inv = ROPE_BASE ** (-jnp.arange(half, dtype=_F32) / half)
    ang_r = (pos // GRID_W).astype(_F32)[:, None] * inv
    ang_c = (pos % GRID_W).astype(_F32)[:, None] * inv
    cos = jnp.concatenate([jnp.cos(ang_r)] * 2 + [jnp.cos(ang_c)] * 2, axis=-1)
    sin_r, sin_c = jnp.sin(ang_r), jnp.sin(ang_c)
    z = jnp.zeros_like(sin_r)
    sa = jnp.concatenate([-sin_r, z, -sin_c, z], axis=-1)
    sb = jnp.concatenate([z, sin_r, z, sin_c], axis=-1)
    cos, sa, sb = (jnp.tile(a, (1, reps)) for a in (cos, sa, sb))
    if pad:
        cos = jnp.concatenate([cos, jnp.ones((seq, pad), _F32)], axis=-1)
        sa = jnp.concatenate([sa, jnp.zeros((seq, pad), _F32)], axis=-1)
        sb = jnp.concatenate([sb, jnp.zeros((seq, pad), _F32)], axis=-1)
    return cos, sa, sb


def _prep_weights(w_in_a, w_q_up, w_kv_up, w_in_d):
    na, d, _ = w_in_a.shape
    ql, kl, fw = Q_LORA, KV_LORA, FNET_GROUPS * FNET_GC
    q, ckv, kr, fz = (w_in_a[..., :ql], w_in_a[..., ql:ql + kl], w_in_a[..., ql + kl:ql + kl + MLA_ROPE],
                      w_in_a[..., ql + kl + MLA_ROPE:])
    w_in = jnp.concatenate([q, ckv, fz, kr, jnp.zeros((na, d, LANES - MLA_ROPE), _F32)], axis=-1).astype(_BF)
    scale = LOG2_E / math.sqrt(MLA_NOPE + MLA_ROPE)
    wq = (w_q_up * scale).reshape(na, ql, MLA_HEADS, MLA_NOPE + MLA_ROPE)
    wq = jnp.pad(wq, ((0, 0), (0, 0), (0, 0), (0, LANES - MLA_ROPE))).reshape(na, ql, -1).astype(_BF)
    wkv = w_kv_up.astype(_BF)
    pw, qk = len(POOL_WINDOWS) * POOL_GC, DIFF_HEADS * 2 * DIFF_DH
    wd = jnp.concatenate([w_in_d[..., :pw], w_in_d[..., pw:pw + qk] * (LOG2_E / math.sqrt(DIFF_DH)),
                          w_in_d[..., pw + qk:]], axis=-1).astype(_BF)
    return w_in, wq, wkv, wd


def _trunk(x, mod, batch, seq, caches, p, tpb_of):
    t, d = x.shape
    depth = p["g_norm"].shape[0]
    decode = caches is not None
    tm = min(TOKEN_TILE, seq) if decode else min(TOKEN_TILE, t)
    tpb = seq // tm if decode else None
    states = ([], [], [], [])
    if decode:
        tab_mla = _rope_tables(seq, 1, LANES - MLA_ROPE)
        tab_diff = _rope_tables(seq, LANES // DIFF_DH, 0)
    else:
        tab_mla = tab_diff = None
    dft_s = _dft_tables(seq)
    dft_c = _dft_tables(FNET_GC)
    aw = MLA_HEADS * MLA_V
    pw = len(POOL_WINDOWS) * POOL_GC
    qk = DIFF_HEADS * 2 * DIFF_DH
    for l in range(depth):
        i = l // 2
        g = p["g_norm"][l]
        x = _ffn_call(x, mod, g[0:1], p["wg"], p["wu"], p["wd"], l, 0, tm, tpb)
        if l % 2 == 0:
            outs = _proj_a_call(x, mod, g[1:2], p["w_in_a"][i], p["g_q"][i:i + 1], p["g_kv"][i:i + 1], p["wq"][i],
                                tab_mla, l, tm, tpb, not decode)
            q, ckv, kr, fz = outs[:4]
            if not decode:
                states[0].append(outs[4])
                states[1].append(outs[5][:, :MLA_ROPE])
            k, v = _kvup_call(ckv, kr, p["wkv"][i], tm)
            kc = vc = None
            if decode:
                past = caches[0].shape[2]
                c_ckv = caches[0][:, i].reshape(batch * past, KV_LORA).astype(_BF)
                c_kr = jnp.pad(caches[1][:, i].reshape(batch * past, MLA_ROPE), ((0, 0), (0, LANES - MLA_ROPE))).astype(_BF)
                kc, vc = _kvup_call(c_ckv, c_kr, p["wkv"][i], min(TOKEN_TILE, past))
            attn = _mla_call(q, k, v, kc, vc, batch, seq)
            four = _fourier_call(fz, dft_s, dft_c, p["w_fnet"][i], batch, seq)
            x = _oproj_call(attn, four, p["w_o_a"][i, :aw], p["w_o_a"][i, aw:], x, mod, l, tm, tpb)
        else:
            lam_init = 0.8 - 0.6 * math.exp(-0.3 * l)
            tm_d = min(PROJ_D_TILE, seq) if decode else tm
            outs = _proj_d_call(x, mod, g[1:2], p["w_in_d"][i], tab_diff, l, tm_d, seq // tm_d if decode else None,
                                not decode)
            pz, qk_new, vt_new = outs[:3]
            if not decode:
                states[2].append(outs[3][:, :qk])
                states[3].append(outs[3][:, qk:])
            kc = vct = None
            if decode:
                past = caches[2].shape[2]
                kc = caches[2][:, i].reshape(batch * past, qk).astype(_BF)
                vct = caches[3][:, i].reshape(batch * past, qk).T.astype(_BF)
            o = _diff_call(qk_new, vt_new, kc, vct, p["lam_qk"][i], p["g_sub"][i:i + 1], batch, seq, lam_init)
            pool = _pool_call(pz, p["w_pool"][i], p["pool_scale"][i:i + 1], batch, seq)
            x = _oproj_call(pool, o, p["w_o_d"][i, :pw], p["w_o_d"][i, pw:], x, mod, l, tm, tpb)
        x = _ffn_call(x, mod, g[2:3], p["wg"], p["wu"], p["wd"], l, 1, tm, tpb,
                      g_final=p["g_final"] if l == depth - 1 else None)
    return x, states


def kernel(x_prompt, x_sample, cache_mla_ckv, cache_mla_krope, cache_diff_k, cache_diff_v, c, c_ctx, w_mod, b_mod, g_norm, w_ffn_gate, w_ffn_up, w_ffn_down, w_in_a, g_q, g_kv, w_q_up, w_kv_up, w_fnet, w_o_a, w_in_d, lam_qk, g_sub, w_pool, pool_scale, w_o_d, g_final):
    batch, seq, d = x_prompt.shape
    dbatch, dseq, _ = x_sample.shape
    assert 1 + dbatch <= MOD_ROWS
    cond = jnp.concatenate([c_ctx[None, :], c, jnp.zeros((MOD_ROWS - 1 - dbatch, d), _F32)], axis=0)
    mod = _mod_call(cond, w_mod, b_mod)

    w_in, wq, wkv, wd_in = _prep_weights(w_in_a, w_q_up, w_kv_up, w_in_d)
    p = dict(g_norm=g_norm, wg=w_ffn_gate.astype(_BF), wu=w_ffn_up.astype(_BF), wd=w_ffn_down.astype(_BF),
             w_in_a=w_in, g_q=g_q, g_kv=g_kv, wq=wq, wkv=wkv, w_fnet=w_fnet.astype(_BF), w_o_a=w_o_a.astype(_BF),
             w_in_d=wd_in, lam_qk=lam_qk, g_sub=g_sub, w_pool=w_pool.astype(_BF), pool_scale=pool_scale,
             w_o_d=w_o_d.astype(_BF), g_final=g_final[None, :])

    y_p, st = _trunk(x_prompt.reshape(batch * seq, d), mod, batch, seq, None, p, None)
    caches = (cache_mla_ckv, cache_mla_krope, cache_diff_k, cache_diff_v)
    y_s, _ = _trunk(x_sample.reshape(dbatch * dseq, d), mod, dbatch, dseq, caches, p, None)

    new_ckv = jnp.stack([s.reshape(batch, seq, KV_LORA) for s in st[0]], axis=1)
    new_kr = jnp.stack([s.reshape(batch, seq, MLA_ROPE) for s in st[1]], axis=1)
    new_k = jnp.stack([s.reshape(batch, seq, DIFF_HEADS, 2, DIFF_DH) for s in st[2]], axis=1)
    new_v = jnp.stack([s.reshape(batch, seq, DIFF_HEADS, 2 * DIFF_DH) for s in st[3]], axis=1)
    return (y_p.reshape(batch, seq, d), y_s.reshape(dbatch, dseq, d), new_ckv, new_kr, new_k, new_v)
```

```python
import functools
import math

import jax
import jax.numpy as jnp
from jax import lax
from jax.experimental import pallas as pl
from jax.experimental.pallas import tpu as pltpu

GRID_W = 64
MLA_HEADS = 12
MLA_NOPE = 128
MLA_ROPE = 64
MLA_V = 128
Q_LORA = 512
KV_LORA = 512
FNET_GROUPS = 4
FNET_GC = 128
POOL_WINDOWS = (2, 4, 8, 16)
POOL_GC = 128
DIFF_HEADS = 12
DIFF_DH = 64
N_MOD = 9
ROPE_BASE = 10000.0
EPS = 1e-6
LOG2_E = math.log2(math.e)

LANES = 128
MXU_DEPTH = 256
VMEM_BYTES = 64 * 2 ** 20
VMEM_LIMIT = VMEM_BYTES - 8 * 2 ** 20

TOKEN_TILE = 512
FFN_TILE = 512
PROJ_D_ROW_CHUNK = 256
PROJ_D_TILE = 1024
MLA_Q_TILE = 256
MLA_UNITS = 8
DIFF_Q_TILE = 256
DIFF_UNITS = 8
ATTN_KEY_CHUNK = 512
MOD_ROWS = 16

_BF = jnp.bfloat16
_F32 = jnp.float32


def _tile(n, pref):
    if n <= pref:
        return n
    t = (pref // LANES) * LANES
    while n % t:
        t -= LANES
    return t


def _params(*sem):
    return pltpu.CompilerParams(dimension_semantics=sem, vmem_limit_bytes=VMEM_LIMIT)


def _dot(a, b):
    return jnp.dot(a, b, preferred_element_type=_F32)


def _dot_t(a, b):
    return lax.dot_general(a, b, (((1,), (1,)), ((), ())), preferred_element_type=_F32)


def _rms(x, g):
    ms = jnp.mean(x * x, axis=-1, keepdims=True)
    return x * lax.rsqrt(ms + EPS) * g


def _norm_mod(x, g, shift, scale):
    w = g * (1.0 + scale)
    ms = jnp.mean(x * x, axis=-1, keepdims=True)
    return (x * lax.rsqrt(ms + EPS)) * w + shift


def _silu(x):
    return x / (1.0 + jnp.exp(-x))


def _rope(x, c, sa, sb):
    return x * c + pltpu.roll(x, LANES - 16, 1) * sa + pltpu.roll(x, 16, 1) * sb


def _mod_row(layer, tiles_per_batch):
    if tiles_per_batch is None:
        return lambda i: layer * MOD_ROWS
    return lambda i: layer * MOD_ROWS + 1 + i // tiles_per_batch


def _mod_kernel(c_ref, w_ref, b_ref, o_ref):
    s = _silu(c_ref[...]).astype(_BF)
    o_ref[...] = _dot(s, w_ref[...].astype(_BF)) + b_ref[...]


def _mod_call(cond, w_mod, b_mod):
    depth, d, n = w_mod.shape
    r = cond.shape[0]
    tn = _tile(n, 1024)
    out = pl.pallas_call(
        _mod_kernel,
        grid=(depth, n // tn),
        in_specs=[pl.BlockSpec((r, d), lambda l, j: (0, 0)),
                  pl.BlockSpec((None, d, tn), lambda l, j: (l, 0, j)),
                  pl.BlockSpec((None, 1, tn), lambda l, j: (l, 0, j))],
        out_specs=pl.BlockSpec((None, r, tn), lambda l, j: (l, 0, j)),
        out_shape=jax.ShapeDtypeStruct((depth, r, n), _F32),
        compiler_params=_params("parallel", "parallel"),
        name="mod",
    )(cond, w_mod, b_mod.reshape(depth, 1, n))
    return out.reshape(depth * r, N_MOD, d)


def _ffn_kernel(x_ref, m_ref, g_ref, wg_ref, wu_ref, wd_ref, gf_ref, o_ref, h_ref, acc_ref, *, base, final):
    j = pl.program_id(1)

    @pl.when(j == 0)
    def _():
        h = _norm_mod(x_ref[...], g_ref[...], m_ref[base:base + 1, :], m_ref[base + 1:base + 2, :])
        h_ref[...] = h.astype(_BF)
        acc_ref[...] = jnp.zeros_like(acc_ref)

    h = h_ref[...]
    a = _silu(_dot(h, wg_ref[...])) * _dot(h, wu_ref[...])
    acc_ref[...] += _dot(a.astype(_BF), wd_ref[...])

    @pl.when(j == pl.num_programs(1) - 1)
    def _():
        y = x_ref[...] + (0.5 * m_ref[base + 2:base + 3, :]) * acc_ref[...]
        if final:
            y = _rms(y, gf_ref[...])
        o_ref[...] = y


def _ffn_call(x, mod, g, wg, wu, wd, layer, sub, tm, tpb, g_final=None):
    t, d = x.shape
    f = wg.shape[-1]
    tf = _tile(f, FFN_TILE)
    row = _mod_row(layer, tpb)
    final = g_final is not None
    gf = g_final if final else g
    return pl.pallas_call(
        functools.partial(_ffn_kernel, base=0 if sub == 0 else 6, final=final),
        grid=(t // tm, f // tf),
        in_specs=[pl.BlockSpec((tm, d), lambda i, j: (i, 0)),
                  pl.BlockSpec((None, N_MOD, d), lambda i, j: (row(i), 0, 0)),
                  pl.BlockSpec((1, d), lambda i, j: (0, 0)),
                  pl.BlockSpec((None, None, d, tf), lambda i, j: (layer, sub, 0, j)),
                  pl.BlockSpec((None, None, d, tf), lambda i, j: (layer, sub, 0, j)),
                  pl.BlockSpec((None, None, tf, d), lambda i, j: (layer, sub, j, 0)),
                  pl.BlockSpec((1, d), lambda i, j: (0, 0))],
        out_specs=pl.BlockSpec((tm, d), lambda i, j: (i, 0)),
        out_shape=jax.ShapeDtypeStruct((t, d), _F32),
        scratch_shapes=[pltpu.VMEM((tm, d), _BF), pltpu.VMEM((tm, d), _F32)],
        compiler_params=_params("parallel", "arbitrary"),
        name="ffn",
    )(x, mod, g, wg, wu, wd, gf)


def _oproj_kernel(a1_ref, a2_ref, w1_ref, w2_ref, x_ref, m_ref, o_ref):
    acc = _dot(a1_ref[...], w1_ref[...]) + _dot(a2_ref[...], w2_ref[...])
    o_ref[...] = x_ref[...] + m_ref[5:6, :] * acc


def _oproj_call(a1, a2, w1, w2, x, mod, layer, tm, tpb):
    t, d = x.shape
    k1, k2 = a1.shape[1], a2.shape[1]
    row = _mod_row(layer, tpb)
    return pl.pallas_call(
        _oproj_kernel,
        grid=(t // tm,),
        in_specs=[pl.BlockSpec((tm, k1), lambda i: (i, 0)),
                  pl.BlockSpec((tm, k2), lambda i: (i, 0)),
                  pl.BlockSpec((k1, d), lambda i: (0, 0)),
                  pl.BlockSpec((k2, d), lambda i: (0, 0)),
                  pl.BlockSpec((tm, d), lambda i: (i, 0)),
                  pl.BlockSpec((None, N_MOD, d), lambda i: (row(i), 0, 0))],
        out_specs=pl.BlockSpec((tm, d), lambda i: (i, 0)),
        out_shape=jax.ShapeDtypeStruct((t, d), _F32),
        compiler_params=_params("parallel"),
        name="oproj",
    )(a1, a2, w1, w2, x, mod)


def _proj_a_kernel(*refs, rope, state, heads):
    x_ref, m_ref, g_ref, win_ref, gq_ref, gkv_ref, wq_ref = refs[:7]
    refs = refs[7:]
    if rope:
        c_ref, sa_ref, sb_ref = refs[:3]
        refs = refs[3:]
    q_ref, ckv_ref, kr_ref, fz_ref = refs[:4]
    h = _norm_mod(x_ref[...], g_ref[...], m_ref[3:4, :], m_ref[4:5, :]).astype(_BF)
    u = _dot(h, win_ref[...])
    ql, kl = Q_LORA, KV_LORA
    fw = FNET_GROUPS * FNET_GC
    qn = _rms(u[:, :ql], gq_ref[...]).astype(_BF)
    ckv = _rms(u[:, ql:ql + kl], gkv_ref[...])
    kr = u[:, ql + kl + fw:]
    ckv_ref[...] = ckv.astype(_BF)
    fz_ref[...] = u[:, ql + kl:ql + kl + fw].astype(_BF)
    if state:
        refs[4][...] = ckv
        refs[5][...] = kr
    q = _dot(qn, wq_ref[...])
    hw = MLA_NOPE + LANES
    if rope:
        c, sa, sb = c_ref[...], sa_ref[...], sb_ref[...]
        kr = _rope(kr, c, sa, sb)
        for hd in range(heads):
            q_ref[:, hd * hw:hd * hw + MLA_NOPE] = q[:, hd * hw:hd * hw + MLA_NOPE].astype(_BF)
            q_ref[:, hd * hw + MLA_NOPE:(hd + 1) * hw] = _rope(q[:, hd * hw + MLA_NOPE:(hd + 1) * hw], c, sa, sb).astype(_BF)
    else:
        q_ref[...] = q.astype(_BF)
    kr_ref[...] = kr.astype(_BF)


def _proj_a_call(x, mod, g, w_in, g_q, g_kv, wq, tables, layer, tm, tpb, state):
    t, d = x.shape
    n_in = w_in.shape[1]
    nq = wq.shape[1]
    fw = FNET_GROUPS * FNET_GC
    rope = tables is not None
    row = _mod_row(layer, tpb)
    in_specs = [pl.BlockSpec((tm, d), lambda i: (i, 0)),
                pl.BlockSpec((None, N_MOD, d), lambda i: (row(i), 0, 0)),
                pl.BlockSpec((1, d), lambda i: (0, 0)),
                pl.BlockSpec((d, n_in), lambda i: (0, 0)),
                pl.BlockSpec((1, Q_LORA), lambda i: (0, 0)),
                pl.BlockSpec((1, KV_LORA), lambda i: (0, 0)),
                pl.BlockSpec((Q_LORA, nq), lambda i: (0, 0))]
    args = [x, mod, g, w_in, g_q, g_kv, wq]
    if rope:
        in_specs += [pl.BlockSpec((tm, LANES), lambda i: (i % tpb, 0))] * 3
        args += list(tables)
    shapes = [((t, nq), _BF), ((t, KV_LORA), _BF), ((t, LANES), _BF), ((t, fw), _BF)]
    if state:
        shapes += [((t, KV_LORA), _F32), ((t, LANES), _F32)]
    return pl.pallas_call(
        functools.partial(_proj_a_kernel, rope=rope, state=state, heads=MLA_HEADS),
        grid=(t // tm,),
        in_specs=in_specs,
        out_specs=[pl.BlockSpec((tm, s[1]), lambda i: (i, 0)) for s, _ in shapes],
        out_shape=[jax.ShapeDtypeStruct(s, dt) for s, dt in shapes],
        compiler_params=_params("parallel"),
        name="proj_a",
    )(*args)


def _kvup_kernel(ckv_ref, kr_ref, w_ref, k_ref, v_ref, *, heads):
    kv = _dot(ckv_ref[...], w_ref[...])
    kr = kr_ref[...]
    hw = MLA_NOPE + MLA_V
    for hd in range(heads):
        k_ref[hd, :, :MLA_NOPE] = kv[:, hd * hw:hd * hw + MLA_NOPE].astype(_BF)
        k_ref[hd, :, MLA_NOPE:] = kr
        v_ref[hd] = kv[:, hd * hw + MLA_NOPE:(hd + 1) * hw].T.astype(_BF)


def _kvup_call(ckv, kr, wkv, tm):
    t = ckv.shape[0]
    heads = MLA_HEADS
    return pl.pallas_call(
        functools.partial(_kvup_kernel, heads=heads),
        grid=(t // tm,),
        in_specs=[pl.BlockSpec((tm, KV_LORA), lambda i: (i, 0)),
                  pl.BlockSpec((tm, LANES), lambda i: (i, 0)),
                  pl.BlockSpec(wkv.shape, lambda i: (0, 0))],
        out_specs=[pl.BlockSpec((heads, tm, MLA_NOPE + LANES), lambda i: (0, i, 0)),
                   pl.BlockSpec((heads, MLA_V, tm), lambda i: (0, 0, i))],
        out_shape=[jax.ShapeDtypeStruct((heads, t, MLA_NOPE + LANES), _BF),
                   jax.ShapeDtypeStruct((heads, MLA_V, t), _BF)],
        compiler_params=_params("parallel"),
        name="kvup",
    )(ckv, kr, wkv)


def _interleave(*stages):
    live = list(stages)
    while live:
        for st in list(live):
            try:
                next(st)
            except StopIteration:
                live.remove(st)


def _key_chunks(seq, past):
    kc = min(ATTN_KEY_CHUNK, seq)
    chunks = [(False, c * kc, kc, c * kc) for c in range(seq // kc)]
    if past:
        pc = min(ATTN_KEY_CHUNK, past)
        chunks += [(True, c * pc, pc, seq + c * pc) for c in range(past // pc)]
    return chunks


def _attn_units(seq, heads, q_tile, n_units):
    tq = min(q_tile, seq)
    nqt = min(n_units, seq // tq)
    hb = math.gcd(heads, max(n_units // nqt, 1))
    return hb, nqt, tq


def _mla_kernel(*refs, hb, nqt, tq, seq, past):
    if past:
        q_ref, k_ref, v_ref, kc_ref, vc_ref, o_ref, *s_refs = refs
    else:
        q_ref, k_ref, v_ref, o_ref, *s_refs = refs
        kc_ref = vc_ref = None
    qw = MLA_NOPE + LANES
    chunks = _key_chunks(seq, past)
    units = [(hd, qt) for hd in range(hb) for qt in range(nqt)]
    col_max = [None] * len(units)

    def scores(u):
        hd, qt = units[u]
        q = q_ref[qt * tq:(qt + 1) * tq, hd * qw:(hd + 1) * qw]
        m = None
        for is_cache, off, n, row in chunks:
            kr = kc_ref if is_cache else k_ref
            s = _dot_t(kr[hd, off:off + n, :], q)
            s_refs[u % 2][row:row + n, :] = s
            mc = jnp.max(s, axis=0, keepdims=True)
            m = mc if m is None else jnp.maximum(m, mc)
            yield
        col_max[u] = m

    def softmax_pv(u):
        hd, qt = units[u]
        m = col_max[u]
        l = acc = None
        for is_cache, off, n, row in chunks:
            vr = vc_ref if is_cache else v_ref
            p = jnp.exp2(s_refs[u % 2][row:row + n, :] - m)
            lc = jnp.sum(p, axis=0, keepdims=True)
            oc = _dot(vr[hd, :, off:off + n], p.astype(_BF))
            l = lc if l is None else l + lc
            acc = oc if acc is None else acc + oc
            yield
        o_ref[qt * tq:(qt + 1) * tq, hd * MLA_V:(hd + 1) * MLA_V] = (acc / l).T.astype(_BF)

    _interleave(scores(0))
    for u in range(len(units)):
        if u + 1 < len(units):
            _interleave(scores(u + 1), softmax_pv(u))
        else:
            _interleave(softmax_pv(u))


def _mla_call(q, k, v, kc, vc, batch, seq):
    t = q.shape[0]
    heads = MLA_HEADS
    hb, nqt, tq = _attn_units(seq, heads, MLA_Q_TILE, MLA_UNITS)
    rows = nqt * tq
    nq = seq // rows
    qw = MLA_NOPE + LANES
    past = 0 if kc is None else kc.shape[1] // batch
    in_specs = [pl.BlockSpec((rows, hb * qw), lambda b, h, i: (b * nq + i, h)),
                pl.BlockSpec((hb, seq, qw), lambda b, h, i: (h, b, 0)),
                pl.BlockSpec((hb, MLA_V, seq), lambda b, h, i: (h, 0, b))]
    args = [q, k, v]
    if past:
        in_specs += [pl.BlockSpec((hb, past, qw), lambda b, h, i: (h, b, 0)),
                     pl.BlockSpec((hb, MLA_V, past), lambda b, h, i: (h, 0, b))]
        args += [kc, vc]
    return pl.pallas_call(
        functools.partial(_mla_kernel, hb=hb, nqt=nqt, tq=tq, seq=seq, past=past),
        grid=(batch, heads // hb, nq),
        in_specs=in_specs,
        out_specs=pl.BlockSpec((rows, hb * MLA_V), lambda b, h, i: (b * nq + i, h)),
        out_shape=jax.ShapeDtypeStruct((t, heads * MLA_V), _BF),
        scratch_shapes=[pltpu.VMEM((seq + past, tq), _F32)] * 2,
        compiler_params=_params("parallel", "parallel", "parallel"),
        name="mla_attn",
    )(*args)


def _fourier_kernel(cs_ref, sn_ref, z_ref, cc_ref, sc_ref, w_ref, o_ref, *, norm):
    z = z_ref[...]
    a = _dot(cs_ref[...], z)
    b = _dot(sn_ref[...], z)
    gc = FNET_GC
    for g in range(FNET_GROUPS):
        f = _dot(a[:, g * gc:(g + 1) * gc].astype(_BF), cc_ref[...]) - _dot(b[:, g * gc:(g + 1) * gc].astype(_BF), sc_ref[...])
        f = (f * norm).astype(_BF)
        o_ref[:, g * gc:(g + 1) * gc] = _dot(f, w_ref[g]).astype(_BF)


def _dft_tables(n):
    r1 = 1 << (int(math.log2(n)) // 2)
    assert n % r1 == 0
    r2 = n // r1

    def unit(rows, period):
        idx = lax.broadcasted_iota(jnp.int32, (rows, n), 0) * lax.broadcasted_iota(jnp.int32, (rows, n), 1) % period
        ang = idx.astype(_F32) * (2.0 * math.pi / period)
        return jnp.cos(ang), jnp.sin(ang)

    ca, sa = (t[:, None, :] for t in unit(r1, r1))
    cb, sb = (t[None, :, :] for t in unit(r2, n))
    cos = (ca * cb - sa * sb).reshape(n, n)
    sin = (sa * cb + ca * sb).reshape(n, n)
    return cos.astype(_BF), sin.astype(_BF)


def _fourier_call(z, tabs_s, tabs_c, w, batch, seq):
    t, fw = z.shape
    tq = min(TOKEN_TILE, seq)
    nq = seq // tq
    cs, sn = tabs_s
    cc, sc = tabs_c
    gc = FNET_GC
    return pl.pallas_call(
        functools.partial(_fourier_kernel, norm=1.0 / math.sqrt(seq * gc)),
        grid=(nq, batch),
        in_specs=[pl.BlockSpec((tq, seq), lambda i, b: (i, 0)),
                  pl.BlockSpec((tq, seq), lambda i, b: (i, 0)),
                  pl.BlockSpec((seq, fw), lambda i, b: (b, 0)),
                  pl.BlockSpec((gc, gc), lambda i, b: (0, 0)),
                  pl.BlockSpec((gc, gc), lambda i, b: (0, 0)),
                  pl.BlockSpec(w.shape, lambda i, b: (0, 0, 0))],
        out_specs=pl.BlockSpec((tq, fw), lambda i, b: (b * nq + i, 0)),
        out_shape=jax.ShapeDtypeStruct((t, fw), _BF),
        compiler_params=_params("parallel", "parallel"),
        name="fourier",
    )(cs, sn, z, cc, sc, w)


def _proj_d_kernel(*refs, rope, state, npz, nqk, chunks):
    x_ref, m_ref, g_ref, w_ref = refs[:4]
    refs = refs[4:]
    if rope:
        c_ref, sa_ref, sb_ref = refs[:3]
        refs = refs[3:]
    pz_ref, qk_ref, vt_ref = refs[:3]
    h_ref = refs[-1]
    j = pl.program_id(1)

    @pl.when(j == 0)
    def _():
        h_ref[...] = _norm_mod(x_ref[...], g_ref[...], m_ref[3:4, :], m_ref[4:5, :]).astype(_BF)

    tm = h_ref.shape[0]
    rc = min(PROJ_D_ROW_CHUNK, tm)
    row_chunks = [slice(r, r + rc) for r in range(0, tm, rc)]

    @pl.when(j < npz)
    def _():
        for rows in row_chunks:
            pz_ref[rows, :] = _dot(h_ref[rows, :], w_ref[...])

    def rotary_tiles(keep):
        for rows in row_chunks:
            acc = _dot(h_ref[rows, :], w_ref[...])
            if keep:
                refs[3][rows, :] = acc
            if rope:
                c, sa, sb = c_ref[rows, :], sa_ref[rows, :], sb_ref[rows, :]
                for k in range(chunks):
                    cols = slice(k * LANES, (k + 1) * LANES)
                    qk_ref[rows, cols] = _rope(acc[:, cols], c, sa, sb).astype(_BF)
            else:
                qk_ref[rows, :] = acc.astype(_BF)

    @pl.when((j >= npz) & (j < npz + nqk))
    def _():
        rotary_tiles(False)

    @pl.when((j >= npz + nqk) & (j < npz + 2 * nqk))
    def _():
        rotary_tiles(state)

    @pl.when(j >= npz + 2 * nqk)
    def _():
        for rows in row_chunks:
            acc = _dot(h_ref[rows, :], w_ref[...])
            if state:
                refs[3][rows, :] = acc
            vt_ref[:, rows] = acc.T.astype(_BF)


def _proj_d_call(x, mod, g, w_in, tables, layer, tm, tpb, state):
    t, d = x.shape
    n = w_in.shape[1]
    pw = len(POOL_WINDOWS) * POOL_GC
    qk = DIFF_HEADS * 2 * DIFF_DH
    tn = math.gcd(math.gcd(pw, qk), 512)
    npz, nqk = pw // tn, qk // tn
    rope = tables is not None
    row = _mod_row(layer, tpb)
    in_specs = [pl.BlockSpec((tm, d), lambda i, j: (i, 0)),
                pl.BlockSpec((None, N_MOD, d), lambda i, j: (row(i), 0, 0)),
                pl.BlockSpec((1, d), lambda i, j: (0, 0)),
                pl.BlockSpec((d, tn), lambda i, j: (0, j))]
    args = [x, mod, g, w_in]
    if rope:
        in_specs += [pl.BlockSpec((tm, LANES), lambda i, j: (i % tpb, 0))] * 3
        args += list(tables)
    out_specs = [pl.BlockSpec((tm, tn), lambda i, j: (i, jnp.minimum(j, npz - 1))),
                 pl.BlockSpec((tm, tn), lambda i, j: (i, jnp.clip(j - npz, 0, 2 * nqk - 1))),
                 pl.BlockSpec((tn, tm), lambda i, j: (jnp.maximum(j - npz - 2 * nqk, 0), i))]
    out_shape = [jax.ShapeDtypeStruct((t, pw), _F32), jax.ShapeDtypeStruct((t, 2 * qk), _BF),
                 jax.ShapeDtypeStruct((qk, t), _BF)]
    if state:
        out_specs.append(pl.BlockSpec((tm, tn), lambda i, j: (i, jnp.maximum(j - npz - nqk, 0))))
        out_shape.append(jax.ShapeDtypeStruct((t, 2 * qk), _F32))
    return pl.pallas_call(
        functools.partial(_proj_d_kernel, rope=rope, state=state, npz=npz, nqk=nqk, chunks=tn // LANES),
        grid=(t // tm, n // tn),
        in_specs=in_specs,
        out_specs=out_specs,
        out_shape=out_shape,
        scratch_shapes=[pltpu.VMEM((tm, d), _BF)],
        compiler_params=_params("parallel", "arbitrary"),
        name="proj_d",
    )(*args)


def _diff_kernel(*refs, lam_init, hb, nqt, tq, seq, past):
    if past:
        q_ref, k_ref, v_ref, kc_ref, vc_ref, lam_ref, g_ref, o_ref, *scr = refs
    else:
        q_ref, k_ref, v_ref, lam_ref, g_ref, o_ref, *scr = refs
        kc_ref = vc_ref = None
    s_refs, e_refs = (scr[0:2], scr[2:4]), scr[4:6]
    lq = lam_ref[...]
    lam = (jnp.exp(jnp.sum(lq[0:1] * lq[1:2], axis=-1, keepdims=True))
           - jnp.exp(jnp.sum(lq[2:3] * lq[3:4], axis=-1, keepdims=True)) + lam_init)
    hw = 2 * DIFF_DH
    lane = lax.broadcasted_iota(jnp.int32, (tq, hw), 1)
    chunks = _key_chunks(seq, past)
    units = [(hd, qt) for hd in range(hb) for qt in range(nqt)]
    col_max = [None] * len(units)
    col_sum = [None] * len(units)

    def scores(u):
        hd, qt = units[u]
        cols = slice(hd * hw, (hd + 1) * hw)
        q = q_ref[qt * tq:(qt + 1) * tq, cols]
        zero = jnp.zeros_like(q)
        halves = (jnp.where(lane < DIFF_DH, q, zero), jnp.where(lane >= DIFF_DH, q, zero))
        m = [None, None]
        for is_cache, off, n, row in chunks:
            kr = kc_ref if is_cache else k_ref
            k = kr[off:off + n, cols]
            for c in range(2):
                s = _dot_t(k, halves[c])
                s_refs[u % 2][c][row:row + n, :] = s
                mc = jnp.max(s, axis=0, keepdims=True)
                m[c] = mc if m[c] is None else jnp.maximum(m[c], mc)
            yield
        col_max[u] = m

    def exponentials(u):
        m = col_max[u]
        l = [None, None]
        for _, _, n, row in chunks:
            for c in range(2):
                e = jnp.exp2(s_refs[u % 2][c][row:row + n, :] - m[c])
                e_refs[c][row:row + n, :] = e
                lc = jnp.sum(e, axis=0, keepdims=True)
                l[c] = lc if l[c] is None else l[c] + lc
            yield
        col_sum[u] = l

    def combine_pv(u):
        hd, qt = units[u]
        cols = slice(hd * hw, (hd + 1) * hw)
        l0, l1 = col_sum[u]
        ratio = lam * l0 / l1
        acc = None
        for is_cache, off, n, row in chunks:
            vr = vc_ref if is_cache else v_ref
            a = (e_refs[0][row:row + n, :] - ratio * e_refs[1][row:row + n, :]).astype(_BF)
            oc = _dot(vr[cols, off:off + n], a)
            acc = oc if acc is None else acc + oc
            yield
        o = (acc / l0).T
        o_ref[qt * tq:(qt + 1) * tq, cols] = (_rms(o, g_ref[...]) * (1.0 - lam_init)).astype(_BF)

    _interleave(scores(0))
    for u in range(len(units)):
        if u + 1 < len(units):
            _interleave(scores(u + 1), exponentials(u))
        else:
            _interleave(exponentials(u))
        _interleave(combine_pv(u))


def _diff_call(qk, vt, kc, vct, lam_qk, g_sub, batch, seq, lam_init):
    t = qk.shape[0]
    heads = DIFF_HEADS
    hw = 2 * DIFF_DH
    hb, nqt, tq = _attn_units(seq, heads, DIFF_Q_TILE, DIFF_UNITS)
    rows = nqt * tq
    nq = seq // rows
    nh = heads // hb
    past = 0 if kc is None else kc.shape[0] // batch
    in_specs = [pl.BlockSpec((rows, hb * hw), lambda b, h, i: (b * nq + i, h)),
                pl.BlockSpec((seq, hb * hw), lambda b, h, i: (b, nh + h)),
                pl.BlockSpec((hb * hw, seq), lambda b, h, i: (h, b))]
    args = [qk, qk, vt]
    if past:
        in_specs += [pl.BlockSpec((past, hb * hw), lambda b, h, i: (b, h)),
                     pl.BlockSpec((hb * hw, past), lambda b, h, i: (h, b))]
        args += [kc, vct]
    in_specs += [pl.BlockSpec(lam_qk.shape, lambda b, h, i: (0, 0)),
                 pl.BlockSpec((1, hw), lambda b, h, i: (0, 0))]
    args += [lam_qk, g_sub]
    return pl.pallas_call(
        functools.partial(_diff_kernel, lam_init=lam_init, hb=hb, nqt=nqt, tq=tq, seq=seq, past=past),
        grid=(batch, nh, nq),
        in_specs=in_specs,
        out_specs=pl.BlockSpec((rows, hb * hw), lambda b, h, i: (b * nq + i, h)),
        out_shape=jax.ShapeDtypeStruct((t, heads * hw), _BF),
        scratch_shapes=[pltpu.VMEM((seq + past, tq), _F32)] * 6,
        compiler_params=_params("parallel", "parallel", "parallel"),
        name="diff_attn",
    )(*args)


POOL_PAD = 16


def _pool_kernel(pz_ref, w_ref, sc_ref, o_ref, pad_ref, *, seq):
    gc = POOL_GC
    t = lax.broadcasted_iota(jnp.int32, (seq, gc), 0)
    zeros = jnp.zeros((POOL_PAD, gc), _F32)
    pad_ref[0:POOL_PAD, :] = zeros
    pad_ref[POOL_PAD + seq:2 * POOL_PAD + seq, :] = zeros
    for g, w in enumerate(POOL_WINDOWS):
        x = pz_ref[:, g * gc:(g + 1) * gc]
        pad_ref[POOL_PAD:POOL_PAD + seq, :] = x
        lo, hi = w // 2, w - w // 2
        tot = pad_ref[POOL_PAD - lo:POOL_PAD - lo + seq, :]
        for dlt in range(-lo + 1, hi):
            tot = tot + pad_ref[POOL_PAD + dlt:POOL_PAD + dlt + seq, :]
        cnt = (jnp.minimum(t + hi, seq) - jnp.maximum(t - lo, 0)).astype(_F32)
        pooled = (tot / cnt - x).astype(_BF)
        y = _dot(pooled, w_ref[g]) * sc_ref[:, g * gc:(g + 1) * gc]
        o_ref[:, g * gc:(g + 1) * gc] = y.astype(_BF)


def _pool_call(pz, w_pool, pool_scale, batch, seq):
    t, pw = pz.shape
    assert max(POOL_WINDOWS) <= POOL_PAD
    return pl.pallas_call(
        functools.partial(_pool_kernel, seq=seq),
        grid=(batch,),
        in_specs=[pl.BlockSpec((seq, pw), lambda b: (b, 0)),
                  pl.BlockSpec(w_pool.shape, lambda b: (0, 0, 0)),
                  pl.BlockSpec((1, pw), lambda b: (0, 0))],
        out_specs=pl.BlockSpec((seq, pw), lambda b: (b, 0)),
        out_shape=jax.ShapeDtypeStruct((t, pw), _BF),
        scratch_shapes=[pltpu.VMEM((seq + 2 * POOL_PAD, POOL_GC), _F32)],
        compiler_params=_params("parallel"),
        name="pool",
    )(pz, w_pool, pool_scale)


def _rope_tables(seq, reps, pad):
    pos = jnp.arange(seq)
    half = MLA_ROPE // 4
    inv = ROPE_BASE ** (-jnp.arange(half, dtype=_F32) / half)
    ang_r = (pos // GRID_W).astype(_F32)[:, None] * inv
    ang_c = (pos % GRID_W).astype(_F32)[:, None] * inv
    cos = jnp.concatenate([jnp.cos(ang_r)] * 2 + [jnp.cos(ang_c)] * 2, axis=-1)
    sin_r, sin_c = jnp.sin(ang_r), jnp.sin(ang_c)
    z = jnp.zeros_like(sin_r)
    sa = jnp.concatenate([-sin_r, z, -sin_c, z], axis=-1)
    sb = jnp.concatenate([z, sin_r, z, sin_c], axis=-1)
    cos, sa, sb = (jnp.tile(a, (1, reps)) for a in (cos, sa, sb))
    if pad:
        cos = jnp.concatenate([cos, jnp.ones((seq, pad), _F32)], axis=-1)
        sa = jnp.concatenate([sa, jnp.zeros((seq, pad), _F32)], axis=-1)
        sb = jnp.concatenate([sb, jnp.zeros((seq, pad), _F32)], axis=-1)
    return cos, sa, sb


def _prep_weights(w_in_a, w_q_up, w_kv_up, w_in_d):
    na, d, _ = w_in_a.shape
    ql, kl, fw = Q_LORA, KV_LORA, FNET_GROUPS * FNET_GC
    q, ckv, kr, fz = (w_in_a[..., :ql], w_in_a[..., ql:ql + kl], w_in_a[..., ql + kl:ql + kl + MLA_ROPE],
                      w_in_a[..., ql + kl + MLA_ROPE:])
    w_in = jnp.concatenate([q, ckv, fz, kr, jnp.zeros((na, d, LANES - MLA_ROPE), _F32)], axis=-1).astype(_BF)
    scale = LOG2_E / math.sqrt(MLA_NOPE + MLA_ROPE)
    wq = (w_q_up * scale).reshape(na, ql, MLA_HEADS, MLA_NOPE + MLA_ROPE)
    wq = jnp.pad(wq, ((0, 0), (0, 0), (0, 0), (0, LANES - MLA_ROPE))).reshape(na, ql, -1).astype(_BF)
    wkv = w_kv_up.astype(_BF)
    pw, qk = len(POOL_WINDOWS) * POOL_GC, DIFF_HEADS * 2 * DIFF_DH
    wd = jnp.concatenate([w_in_d[..., :pw], w_in_d[..., pw:pw + qk] * (LOG2_E / math.sqrt(DIFF_DH)),
                          w_in_d[..., pw + qk:]], axis=-1).astype(_BF)
    return w_in, wq, wkv, wd


def _trunk(x, mod, batch, seq, caches, p, tpb_of):
    t, d = x.shape
    depth = p["g_norm"].shape[0]
    decode = caches is not None
    tm = min(TOKEN_TILE, seq) if decode else min(TOKEN_TILE, t)
    tpb = seq // tm if decode else None
    states = ([], [], [], [])
    if decode:
        tab_mla = _rope_tables(seq, 1, LANES - MLA_ROPE)
        tab_diff = _rope_tables(seq, LANES // DIFF_DH, 0)
    else:
        tab_mla = tab_diff = None
    dft_s = _dft_tables(seq)
    dft_c = _dft_tables(FNET_GC)
    aw = MLA_HEADS * MLA_V
    pw = len(POOL_WINDOWS) * POOL_GC
    qk = DIFF_HEADS * 2 * DIFF_DH
    for l in range(depth):
        i = l // 2
        g = p["g_norm"][l]
        x = _ffn_call(x, mod, g[0:1], p["wg"], p["wu"], p["wd"], l, 0, tm, tpb)
        if l % 2 == 0:
            outs = _proj_a_call(x, mod, g[1:2], p["w_in_a"][i], p["g_q"][i:i + 1], p["g_kv"][i:i + 1], p["wq"][i],
                                tab_mla, l, tm, tpb, not decode)
            q, ckv, kr, fz = outs[:4]
            if not decode:
                states[0].append(outs[4])
                states[1].append(outs[5][:, :MLA_ROPE])
            k, v = _kvup_call(ckv, kr, p["wkv"][i], tm)
            kc = vc = None
            if decode:
                past = caches[0].shape[2]
                c_ckv = caches[0][:, i].reshape(batch * past, KV_LORA).astype(_BF)
                c_kr = jnp.pad(caches[1][:, i].reshape(batch * past, MLA_ROPE), ((0, 0), (0, LANES - MLA_ROPE))).astype(_BF)
                kc, vc = _kvup_call(c_ckv, c_kr, p["wkv"][i], min(TOKEN_TILE, past))
            attn = _mla_call(q, k, v, kc, vc, batch, seq)
            four = _fourier_call(fz, dft_s, dft_c, p["w_fnet"][i], batch, seq)
            x = _oproj_call(attn, four, p["w_o_a"][i, :aw], p["w_o_a"][i, aw:], x, mod, l, tm, tpb)
        else:
            lam_init = 0.8 - 0.6 * math.exp(-0.3 * l)
            tm_d = min(PROJ_D_TILE, seq) if decode else tm
            outs = _proj_d_call(x, mod, g[1:2], p["w_in_d"][i], tab_diff, l, tm_d, seq // tm_d if decode else None,
                                not decode)
            pz, qk_new, vt_new = outs[:3]
            if not decode:
                states[2].append(outs[3][:, :qk])
                states[3].append(outs[3][:, qk:])
            kc = vct = None
            if decode:
                past = caches[2].shape[2]
                kc = caches[2][:, i].reshape(batch * past, qk).astype(_BF)
                vct = caches[3][:, i].reshape(batch * past, qk).T.astype(_BF)
            o = _diff_call(qk_new, vt_new, kc, vct, p["lam_qk"][i], p["g_sub"][i:i + 1], batch, seq, lam_init)
            pool = _pool_call(pz, p["w_pool"][i], p["pool_scale"][i:i + 1], batch, seq)
            x = _oproj_call(pool, o, p["w_o_d"][i, :pw], p["w_o_d"][i, pw:], x, mod, l, tm, tpb)
        x = _ffn_call(x, mod, g[2:3], p["wg"], p["wu"], p["wd"], l, 1, tm, tpb,
                      g_final=p["g_final"] if l == depth - 1 else None)
    return x, states


def kernel(x_prompt, x_sample, cache_mla_ckv, cache_mla_krope, cache_diff_k, cache_diff_v, c, c_ctx, w_mod, b_mod, g_norm, w_ffn_gate, w_ffn_up, w_ffn_down, w_in_a, g_q, g_kv, w_q_up, w_kv_up, w_fnet, w_o_a, w_in_d, lam_qk, g_sub, w_pool, pool_scale, w_o_d, g_final):
    batch, seq, d = x_prompt.shape
    dbatch, dseq, _ = x_sample.shape
    assert 1 + dbatch <= MOD_ROWS
    cond = jnp.concatenate([c_ctx[None, :], c, jnp.zeros((MOD_ROWS - 1 - dbatch, d), _F32)], axis=0)
    mod = _mod_call(cond, w_mod, b_mod)

    w_in, wq, wkv, wd_in = _prep_weights(w_in_a, w_q_up, w_kv_up, w_in_d)
    p = dict(g_norm=g_norm, wg=w_ffn_gate.astype(_BF), wu=w_ffn_up.astype(_BF), wd=w_ffn_down.astype(_BF),
             w_in_a=w_in, g_q=g_q, g_kv=g_kv, wq=wq, wkv=wkv, w_fnet=w_fnet.astype(_BF), w_o_a=w_o_a.astype(_BF),
             w_in_d=wd_in, lam_qk=lam_qk, g_sub=g_sub, w_pool=w_pool.astype(_BF), pool_scale=pool_scale,
             w_o_d=w_o_d.astype(_BF), g_final=g_final[None, :])

    y_p, st = _trunk(x_prompt.reshape(batch * seq, d), mod, batch, seq, None, p, None)
    caches = (cache_mla_ckv, cache_mla_krope, cache_diff_k, cache_diff_v)
    y_s, _ = _trunk(x_sample.reshape(dbatch * dseq, d), mod, dbatch, dseq, caches, p, None)

    new_ckv = jnp.stack([s.reshape(batch, seq, KV_LORA) for s in st[0]], axis=1)
    new_kr = jnp.stack([s.reshape(batch, seq, MLA_ROPE) for s in st[1]], axis=1)
    new_k = jnp.stack([s.reshape(batch, seq, DIFF_HEADS, 2, DIFF_DH) for s in st[2]], axis=1)
    new_v = jnp.stack([s.reshape(batch, seq, DIFF_HEADS, 2 * DIFF_DH) for s in st[3]], axis=1)
    return (y_p.reshape(batch, seq, d), y_s.reshape(dbatch, dseq, d), new_ckv, new_kr, new_k, new_v)
```

```python
import functools
import math

import jax
import jax.numpy as jnp
from jax import lax
from jax.experimental import pallas as pl
from jax.experimental.pallas import tpu as pltpu

GRID_W = 64
MLA_HEADS = 12
MLA_NOPE = 128
MLA_ROPE = 64
MLA_V = 128
Q_LORA = 512
KV_LORA = 512
FNET_GROUPS = 4
FNET_GC = 128
POOL_WINDOWS = (2, 4, 8, 16)
POOL_GC = 128
DIFF_HEADS = 12
DIFF_DH = 64
N_MOD = 9
ROPE_BASE = 10000.0
EPS = 1e-6
LOG2_E = math.log2(math.e)
ROT_SPAN = MLA_ROPE // 4
assert DIFF_DH == MLA_ROPE

LANES = 128
VMEM_BYTES = 64 * 2 ** 20
VMEM_LIMIT = VMEM_BYTES - 8 * 2 ** 20

TOKEN_TILE = 512
FFN_TILE = 512
PROJ_D_ROW_CHUNK = 256
PROJ_D_TILE = 1024
MLA_Q_TILE = 256
MLA_UNITS = 8
DIFF_Q_TILE = 256
DIFF_UNITS = 8
ATTN_KEY_CHUNK = 512
MOD_ROWS = 16

_BF = jnp.bfloat16
_F32 = jnp.float32


def _tile(n, pref):
    if n <= pref:
        return n
    t = (pref // LANES) * LANES
    while n % t:
        t -= LANES
    return t


def _params(*sem):
    return pltpu.CompilerParams(dimension_semantics=sem, vmem_limit_bytes=VMEM_LIMIT)


def _dot(a, b):
    return jnp.dot(a, b, preferred_element_type=_F32)


def _dot_t(a, b):
    return lax.dot_general(a, b, (((1,), (1,)), ((), ())), preferred_element_type=_F32)


def _rms(x, g):
    ms = jnp.mean(x * x, axis=-1, keepdims=True)
    return x * lax.rsqrt(ms + EPS) * g


def _norm_mod(x, g, shift, scale):
    w = g * (1.0 + scale)
    ms = jnp.mean(x * x, axis=-1, keepdims=True)
    return (x * lax.rsqrt(ms + EPS)) * w + shift


def _silu(x):
    return x / (1.0 + jnp.exp(-x))


def _rope(x, c, sa, sb):
    return x * c + pltpu.roll(x, LANES - ROT_SPAN, 1) * sa + pltpu.roll(x, ROT_SPAN, 1) * sb


def _mod_row(layer, tiles_per_batch):
    if tiles_per_batch is None:
        return lambda i: layer * MOD_ROWS
    return lambda i: layer * MOD_ROWS + 1 + i // tiles_per_batch


def _mod_kernel(c_ref, w_ref, b_ref, o_ref):
    s = _silu(c_ref[...]).astype(_BF)
    o_ref[...] = _dot(s, w_ref[...].astype(_BF)) + b_ref[...]


def _mod_call(cond, w_mod, b_mod):
    depth, d, n = w_mod.shape
    r = cond.shape[0]
    tn = _tile(n, 1024)
    out = pl.pallas_call(
        _mod_kernel,
        grid=(depth, n // tn),
        in_specs=[pl.BlockSpec((r, d), lambda l, j: (0, 0)),
                  pl.BlockSpec((None, d, tn), lambda l, j: (l, 0, j)),
                  pl.BlockSpec((None, 1, tn), lambda l, j: (l, 0, j))],
        out_specs=pl.BlockSpec((None, r, tn), lambda l, j: (l, 0, j)),
        out_shape=jax.ShapeDtypeStruct((depth, r, n), _F32),
        compiler_params=_params("parallel", "parallel"),
        name="mod",
    )(cond, w_mod, b_mod.reshape(depth, 1, n))
    return out.reshape(depth * r, N_MOD, d)


def _ffn_kernel(x_ref, m_ref, g_ref, wg_ref, wu_ref, wd_ref, gf_ref, o_ref, h_ref, acc_ref, *, base, final):
    j = pl.program_id(1)

    @pl.when(j == 0)
    def _():
        h = _norm_mod(x_ref[...], g_ref[...], m_ref[base:base + 1, :], m_ref[base + 1:base + 2, :])
        h_ref[...] = h.astype(_BF)
        acc_ref[...] = jnp.zeros_like(acc_ref)

    h = h_ref[...]
    a = _silu(_dot(h, wg_ref[...])) * _dot(h, wu_ref[...])
    acc_ref[...] += _dot(a.astype(_BF), wd_ref[...])

    @pl.when(j == pl.num_programs(1) - 1)
    def _():
        y = x_ref[...] + (0.5 * m_ref[base + 2:base + 3, :]) * acc_ref[...]
        if final:
            y = _rms(y, gf_ref[...])
        o_ref[...] = y


def _ffn_call(x, mod, g, wg, wu, wd, layer, sub, tm, tpb, g_final=None):
    t, d = x.shape
    f = wg.shape[-1]
    tf = _tile(f, FFN_TILE)
    row = _mod_row(layer, tpb)
    final = g_final is not None
    gf = g_final if final else g
    return pl.pallas_call(
        functools.partial(_ffn_kernel, base=0 if sub == 0 else 6, final=final),
        grid=(t // tm, f // tf),
        in_specs=[pl.BlockSpec((tm, d), lambda i, j: (i, 0)),
                  pl.BlockSpec((None, N_MOD, d), lambda i, j: (row(i), 0, 0)),
                  pl.BlockSpec((1, d), lambda i, j: (0, 0)),
                  pl.BlockSpec((None, None, d, tf), lambda i, j: (layer, sub, 0, j)),
                  pl.BlockSpec((None, None, d, tf), lambda i, j: (layer, sub, 0, j)),
                  pl.BlockSpec((None, None, tf, d), lambda i, j: (layer, sub, j, 0)),
                  pl.BlockSpec((1, d), lambda i, j: (0, 0))],
        out_specs=pl.BlockSpec((tm, d), lambda i, j: (i, 0)),
        out_shape=jax.ShapeDtypeStruct((t, d), _F32),
        scratch_shapes=[pltpu.VMEM((tm, d), _BF), pltpu.VMEM((tm, d), _F32)],
        compiler_params=_params("parallel", "arbitrary"),
        name="ffn",
    )(x, mod, g, wg, wu, wd, gf)


def _oproj_kernel(a1_ref, a2_ref, w1_ref, w2_ref, x_ref, m_ref, o_ref):
    acc = _dot(a1_ref[...], w1_ref[...]) + _dot(a2_ref[...], w2_ref[...])
    o_ref[...] = x_ref[...] + m_ref[5:6, :] * acc


def _oproj_call(a1, a2, w1, w2, x, mod, layer, tm, tpb):
    t, d = x.shape
    k1, k2 = a1.shape[1], a2.shape[1]
    row = _mod_row(layer, tpb)
    return pl.pallas_call(
        _oproj_kernel,
        grid=(t // tm,),
        in_specs=[pl.BlockSpec((tm, k1), lambda i: (i, 0)),
                  pl.BlockSpec((tm, k2), lambda i: (i, 0)),
                  pl.BlockSpec((k1, d), lambda i: (0, 0)),
                  pl.BlockSpec((k2, d), lambda i: (0, 0)),
                  pl.BlockSpec((tm, d), lambda i: (i, 0)),
                  pl.BlockSpec((None, N_MOD, d), lambda i: (row(i), 0, 0))],
        out_specs=pl.BlockSpec((tm, d), lambda i: (i, 0)),
        out_shape=jax.ShapeDtypeStruct((t, d), _F32),
        compiler_params=_params("parallel"),
        name="oproj",
    )(a1, a2, w1, w2, x, mod)


def _proj_a_kernel(*refs, rope, state, heads):
    x_ref, m_ref, g_ref, win_ref, gq_ref, gkv_ref, wq_ref = refs[:7]
    refs = refs[7:]
    if rope:
        c_ref, sa_ref, sb_ref = refs[:3]
        refs = refs[3:]
    q_ref, ckv_ref, kr_ref, fz_ref = refs[:4]
    h = _norm_mod(x_ref[...], g_ref[...], m_ref[3:4, :], m_ref[4:5, :]).astype(_BF)
    u = _dot(h, win_ref[...])
    ql, kl = Q_LORA, KV_LORA
    fw = FNET_GROUPS * FNET_GC
    qn = _rms(u[:, :ql], gq_ref[...]).astype(_BF)
    ckv = _rms(u[:, ql:ql + kl], gkv_ref[...])
    kr = u[:, ql + kl + fw:]
    ckv_ref[...] = ckv.astype(_BF)
    fz_ref[...] = u[:, ql + kl:ql + kl + fw].astype(_BF)
    if state:
        refs[4][...] = ckv
        refs[5][...] = kr
    q = _dot(qn, wq_ref[...])
    hw = MLA_NOPE + LANES
    if rope:
        c, sa, sb = c_ref[...], sa_ref[...], sb_ref[...]
        kr = _rope(kr, c, sa, sb)
        for hd in range(heads):
            q_ref[:, hd * hw:hd * hw + MLA_NOPE] = q[:, hd * hw:hd * hw + MLA_NOPE].astype(_BF)
            q_ref[:, hd * hw + MLA_NOPE:(hd + 1) * hw] = _rope(q[:, hd * hw + MLA_NOPE:(hd + 1) * hw], c, sa, sb).astype(_BF)
    else:
        q_ref[...] = q.astype(_BF)
    kr_ref[...] = kr.astype(_BF)


def _proj_a_call(x, mod, g, w_in, g_q, g_kv, wq, tables, layer, tm, tpb, state):
    t, d = x.shape
    n_in = w_in.shape[1]
    nq = wq.shape[1]
    fw = FNET_GROUPS * FNET_GC
    rope = tables is not None
    row = _mod_row(layer, tpb)
    in_specs = [pl.BlockSpec((tm, d), lambda i: (i, 0)),
                pl.BlockSpec((None, N_MOD, d), lambda i: (row(i), 0, 0)),
                pl.BlockSpec((1, d), lambda i: (0, 0)),
                pl.BlockSpec((d, n_in), lambda i: (0, 0)),
                pl.BlockSpec((1, Q_LORA), lambda i: (0, 0)),
                pl.BlockSpec((1, KV_LORA), lambda i: (0, 0)),
                pl.BlockSpec((Q_LORA, nq), lambda i: (0, 0))]
    args = [x, mod, g, w_in, g_q, g_kv, wq]
    if rope:
        in_specs += [pl.BlockSpec((tm, LANES), lambda i: (i % tpb, 0))] * 3
        args += list(tables)
    shapes = [((t, nq), _BF), ((t, KV_LORA), _BF), ((t, LANES), _BF), ((t, fw), _BF)]
    if state:
        shapes += [((t, KV_LORA), _F32), ((t, LANES), _F32)]
    return pl.pallas_call(
        functools.partial(_proj_a_kernel, rope=rope, state=state, heads=MLA_HEADS),
        grid=(t // tm,),
        in_specs=in_specs,
        out_specs=[pl.BlockSpec((tm, s[1]), lambda i: (i, 0)) for s, _ in shapes],
        out_shape=[jax.ShapeDtypeStruct(s, dt) for s, dt in shapes],
        compiler_params=_params("parallel"),
        name="proj_a",
    )(*args)


def _kvup_kernel(ckv_ref, kr_ref, w_ref, k_ref, v_ref, *, heads):
    kv = _dot(ckv_ref[...], w_ref[...])
    kr = kr_ref[...]
    hw = MLA_NOPE + MLA_V
    for hd in range(heads):
        k_ref[hd, :, :MLA_NOPE] = kv[:, hd * hw:hd * hw + MLA_NOPE].astype(_BF)
        k_ref[hd, :, MLA_NOPE:] = kr
        v_ref[hd] = kv[:, hd * hw + MLA_NOPE:(hd + 1) * hw].T.astype(_BF)


def _kvup_call(ckv, kr, wkv, tm):
    t = ckv.shape[0]
    heads = MLA_HEADS
    return pl.pallas_call(
        functools.partial(_kvup_kernel, heads=heads),
        grid=(t // tm,),
        in_specs=[pl.BlockSpec((tm, KV_LORA), lambda i: (i, 0)),
                  pl.BlockSpec((tm, LANES), lambda i: (i, 0)),
                  pl.BlockSpec(wkv.shape, lambda i: (0, 0))],
        out_specs=[pl.BlockSpec((heads, tm, MLA_NOPE + LANES), lambda i: (0, i, 0)),
                   pl.BlockSpec((heads, MLA_V, tm), lambda i: (0, 0, i))],
        out_shape=[jax.ShapeDtypeStruct((heads, t, MLA_NOPE + LANES), _BF),
                   jax.ShapeDtypeStruct((heads, MLA_V, t), _BF)],
        compiler_params=_params("parallel"),
        name="kvup",
    )(ckv, kr, wkv)


def _interleave(*stages):
    live = list(stages)
    while live:
        for st in list(live):
            try:
                next(st)
            except StopIteration:
                live.remove(st)


def _key_chunks(seq, past):
    kc = min(ATTN_KEY_CHUNK, seq)
    chunks = [(False, c * kc, kc, c * kc) for c in range(seq // kc)]
    if past:
        pc = min(ATTN_KEY_CHUNK, past)
        chunks += [(True, c * pc, pc, seq + c * pc) for c in range(past // pc)]
    return chunks


def _attn_units(seq, heads, q_tile, n_units):
    tq = min(q_tile, seq)
    nqt = min(n_units, seq // tq)
    hb = math.gcd(heads, max(n_units // nqt, 1))
    return hb, nqt, tq


def _mla_kernel(*refs, hb, nqt, tq, seq, past):
    if past:
        q_ref, k_ref, v_ref, kc_ref, vc_ref, o_ref, *s_refs = refs
    else:
        q_ref, k_ref, v_ref, o_ref, *s_refs = refs
        kc_ref = vc_ref = None
    qw = MLA_NOPE + LANES
    chunks = _key_chunks(seq, past)
    units = [(hd, qt) for hd in range(hb) for qt in range(nqt)]
    col_max = [None] * len(units)

    def scores(u):
        hd, qt = units[u]
        q = q_ref[qt * tq:(qt + 1) * tq, hd * qw:(hd + 1) * qw]
        m = None
        for is_cache, off, n, row in chunks:
            kr = kc_ref if is_cache else k_ref
            s = _dot_t(kr[hd, off:off + n, :], q)
            s_refs[u % 2][row:row + n, :] = s
            mc = jnp.max(s, axis=0, keepdims=True)
            m = mc if m is None else jnp.maximum(m, mc)
            yield
        col_max[u] = m

    def softmax_pv(u):
        hd, qt = units[u]
        m = col_max[u]
        l = acc = None
        for is_cache, off, n, row in chunks:
            vr = vc_ref if is_cache else v_ref
            p = jnp.exp2(s_refs[u % 2][row:row + n, :] - m)
            lc = jnp.sum(p, axis=0, keepdims=True)
            oc = _dot(vr[hd, :, off:off + n], p.astype(_BF))
            l = lc if l is None else l + lc
            acc = oc if acc is None else acc + oc
            yield
        o_ref[qt * tq:(qt + 1) * tq, hd * MLA_V:(hd + 1) * MLA_V] = (acc / l).T.astype(_BF)

    _interleave(scores(0))
    for u in range(len(units)):
        if u + 1 < len(units):
            _interleave(scores(u + 1), softmax_pv(u))
        else:
            _interleave(softmax_pv(u))


def _mla_call(q, k, v, kc, vc, batch, seq):
    t = q.shape[0]
    heads = MLA_HEADS
    hb, nqt, tq = _attn_units(seq, heads, MLA_Q_TILE, MLA_UNITS)
    rows = nqt * tq
    nq = seq // rows
    qw = MLA_NOPE + LANES
    past = 0 if kc is None else kc.shape[1] // batch
    in_specs = [pl.BlockSpec((rows, hb * qw), lambda b, h, i: (b * nq + i, h)),
                pl.BlockSpec((hb, seq, qw), lambda b, h, i: (h, b, 0)),
                pl.BlockSpec((hb, MLA_V, seq), lambda b, h, i: (h, 0, b))]
    args = [q, k, v]
    if past:
        in_specs += [pl.BlockSpec((hb, past, qw), lambda b, h, i: (h, b, 0)),
                     pl.BlockSpec((hb, MLA_V, past), lambda b, h, i: (h, 0, b))]
        args += [kc, vc]
    return pl.pallas_call(
        functools.partial(_mla_kernel, hb=hb, nqt=nqt, tq=tq, seq=seq, past=past),
        grid=(batch, heads // hb, nq),
        in_specs=in_specs,
        out_specs=pl.BlockSpec((rows, hb * MLA_V), lambda b, h, i: (b * nq + i, h)),
        out_shape=jax.ShapeDtypeStruct((t, heads * MLA_V), _BF),
        scratch_shapes=[pltpu.VMEM((seq + past, tq), _F32)] * 2,
        compiler_params=_params("parallel", "parallel", "parallel"),
        name="mla_attn",
    )(*args)


def _fourier_kernel(cs_ref, sn_ref, z_ref, cc_ref, sc_ref, w_ref, o_ref, *, norm):
    z = z_ref[...]
    a = _dot(cs_ref[...], z)
    b = _dot(sn_ref[...], z)
    gc = FNET_GC
    for g in range(FNET_GROUPS):
        f = _dot(a[:, g * gc:(g + 1) * gc].astype(_BF), cc_ref[...]) - _dot(b[:, g * gc:(g + 1) * gc].astype(_BF), sc_ref[...])
        f = (f * norm).astype(_BF)
        o_ref[:, g * gc:(g + 1) * gc] = _dot(f, w_ref[g]).astype(_BF)


def _dft_tables(n):
    r1 = 1 << (int(math.log2(n)) // 2)
    assert n % r1 == 0
    r2 = n // r1

    def unit(rows, period):
        idx = lax.broadcasted_iota(jnp.int32, (rows, n), 0) * lax.broadcasted_iota(jnp.int32, (rows, n), 1) % period
        ang = idx.astype(_F32) * (2.0 * math.pi / period)
        return jnp.cos(ang), jnp.sin(ang)

    ca, sa = (t[:, None, :] for t in unit(r1, r1))
    cb, sb = (t[None, :, :] for t in unit(r2, n))
    cos = (ca * cb - sa * sb).reshape(n, n)
    sin = (sa * cb + ca * sb).reshape(n, n)
    return cos.astype(_BF), sin.astype(_BF)


def _fourier_call(z, tabs_s, tabs_c, w, batch, seq):
    t, fw = z.shape
    tq = min(TOKEN_TILE, seq)
    nq = seq // tq
    cs, sn = tabs_s
    cc, sc = tabs_c
    gc = FNET_GC
    return pl.pallas_call(
        functools.partial(_fourier_kernel, norm=1.0 / math.sqrt(seq * gc)),
        grid=(nq, batch),
        in_specs=[pl.BlockSpec((tq, seq), lambda i, b: (i, 0)),
                  pl.BlockSpec((tq, seq), lambda i, b: (i, 0)),
                  pl.BlockSpec((seq, fw), lambda i, b: (b, 0)),
                  pl.BlockSpec((gc, gc), lambda i, b: (0, 0)),
                  pl.BlockSpec((gc, gc), lambda i, b: (0, 0)),
                  pl.BlockSpec(w.shape, lambda i, b: (0, 0, 0))],
        out_specs=pl.BlockSpec((tq, fw), lambda i, b: (b * nq + i, 0)),
        out_shape=jax.ShapeDtypeStruct((t, fw), _BF),
        compiler_params=_params("parallel", "parallel"),
        name="fourier",
    )(cs, sn, z, cc, sc, w)


def _proj_d_kernel(*refs, rope, state, npz, nqk, chunks):
    x_ref, m_ref, g_ref, w_ref = refs[:4]
    refs = refs[4:]
    if rope:
        c_ref, sa_ref, sb_ref = refs[:3]
        refs = refs[3:]
    pz_ref, qk_ref, vt_ref = refs[:3]
    h_ref = refs[-1]
    j = pl.program_id(1)

    @pl.when(j == 0)
    def _():
        h_ref[...] = _norm_mod(x_ref[...], g_ref[...], m_ref[3:4, :], m_ref[4:5, :]).astype(_BF)

    tm = h_ref.shape[0]
    rc = min(PROJ_D_ROW_CHUNK, tm)
    row_chunks = [slice(r, r + rc) for r in range(0, tm, rc)]

    @pl.when(j < npz)
    def _():
        for rows in row_chunks:
            pz_ref[rows, :] = _dot(h_ref[rows, :], w_ref[...])

    def rotary_tiles(keep):
        for rows in row_chunks:
            acc = _dot(h_ref[rows, :], w_ref[...])
            if keep:
                refs[3][rows, :] = acc
            if rope:
                c, sa, sb = c_ref[rows, :], sa_ref[rows, :], sb_ref[rows, :]
                for k in range(chunks):
                    cols = slice(k * LANES, (k + 1) * LANES)
                    qk_ref[rows, cols] = _rope(acc[:, cols], c, sa, sb).astype(_BF)
            else:
                qk_ref[rows, :] = acc.astype(_BF)

    @pl.when((j >= npz) & (j < npz + nqk))
    def _():
        rotary_tiles(False)

    @pl.when((j >= npz + nqk) & (j < npz + 2 * nqk))
    def _():
        rotary_tiles(state)

    @pl.when(j >= npz + 2 * nqk)
    def _():
        for rows in row_chunks:
            acc = _dot(h_ref[rows, :], w_ref[...])
            if state:
                refs[3][rows, :] = acc
            vt_ref[:, rows] = acc.T.astype(_BF)


def _proj_d_call(x, mod, g, w_in, tables, layer, tm, tpb, state):
    t, d = x.shape
    n = w_in.shape[1]
    pw = len(POOL_WINDOWS) * POOL_GC
    qk = DIFF_HEADS * 2 * DIFF_DH
    tn = math.gcd(math.gcd(pw, qk), 512)
    npz, nqk = pw // tn, qk // tn
    rope = tables is not None
    row = _mod_row(layer, tpb)
    in_specs = [pl.BlockSpec((tm, d), lambda i, j: (i, 0)),
                pl.BlockSpec((None, N_MOD, d), lambda i, j: (row(i), 0, 0)),
                pl.BlockSpec((1, d), lambda i, j: (0, 0)),
                pl.BlockSpec((d, tn), lambda i, j: (0, j))]
    args = [x, mod, g, w_in]
    if rope:
        in_specs += [pl.BlockSpec((tm, LANES), lambda i, j: (i % tpb, 0))] * 3
        args += list(tables)
    out_specs = [pl.BlockSpec((tm, tn), lambda i, j: (i, jnp.minimum(j, npz - 1))),
                 pl.BlockSpec((tm, tn), lambda i, j: (i, jnp.clip(j - npz, 0, 2 * nqk - 1))),
                 pl.BlockSpec((tn, tm), lambda i, j: (jnp.maximum(j - npz - 2 * nqk, 0), i))]
    out_shape = [jax.ShapeDtypeStruct((t, pw), _F32), jax.ShapeDtypeStruct((t, 2 * qk), _BF),
                 jax.ShapeDtypeStruct((qk, t), _BF)]
    if state:
        out_specs.append(pl.BlockSpec((tm, tn), lambda i, j: (i, jnp.maximum(j - npz - nqk, 0))))
        out_shape.append(jax.ShapeDtypeStruct((t, 2 * qk), _F32))
    return pl.pallas_call(
        functools.partial(_proj_d_kernel, rope=rope, state=state, npz=npz, nqk=nqk, chunks=tn // LANES),
        grid=(t // tm, n // tn),
        in_specs=in_specs,
        out_specs=out_specs,
        out_shape=out_shape,
        scratch_shapes=[pltpu.VMEM((tm, d), _BF)],
        compiler_params=_params("parallel", "arbitrary"),
        name="proj_d",
    )(*args)


def _diff_kernel(*refs, lam_init, hb, nqt, tq, seq, past):
    if past:
        q_ref, k_ref, v_ref, kc_ref, vc_ref, lam_ref, g_ref, o_ref, *scr = refs
    else:
        q_ref, k_ref, v_ref, lam_ref, g_ref, o_ref, *scr = refs
        kc_ref = vc_ref = None
    s_refs, e_refs = (scr[0:2], scr[2:4]), scr[4:6]
    lq = lam_ref[...]
    lam = (jnp.exp(jnp.sum(lq[0:1] * lq[1:2], axis=-1, keepdims=True))
           - jnp.exp(jnp.sum(lq[2:3] * lq[3:4], axis=-1, keepdims=True)) + lam_init)
    hw = 2 * DIFF_DH
    lane = lax.broadcasted_iota(jnp.int32, (tq, hw), 1)
    chunks = _key_chunks(seq, past)
    units = [(hd, qt) for hd in range(hb) for qt in range(nqt)]
    col_max = [None] * len(units)
    col_sum = [None] * len(units)

    def scores(u):
        hd, qt = units[u]
        cols = slice(hd * hw, (hd + 1) * hw)
        q = q_ref[qt * tq:(qt + 1) * tq, cols]
        zero = jnp.zeros_like(q)
        halves = (jnp.where(lane < DIFF_DH, q, zero), jnp.where(lane >= DIFF_DH, q, zero))
        m = [None, None]
        for is_cache, off, n, row in chunks:
            kr = kc_ref if is_cache else k_ref
            k = kr[off:off + n, cols]
            for c in range(2):
                s = _dot_t(k, halves[c])
                s_refs[u % 2][c][row:row + n, :] = s
                mc = jnp.max(s, axis=0, keepdims=True)
                m[c] = mc if m[c] is None else jnp.maximum(m[c], mc)
            yield
        col_max[u] = m

    def exponentials(u):
        m = col_max[u]
        l = [None, None]
        for _, _, n, row in chunks:
            for c in range(2):
                e = jnp.exp2(s_refs[u % 2][c][row:row + n, :] - m[c])
                e_refs[c][row:row + n, :] = e
                lc = jnp.sum(e, axis=0, keepdims=True)
                l[c] = lc if l[c] is None else l[c] + lc
            yield
        col_sum[u] = l

    def combine_pv(u):
        hd, qt = units[u]
        cols = slice(hd * hw, (hd + 1) * hw)
        l0, l1 = col_sum[u]
        ratio = lam * l0 / l1
        acc = None
        for is_cache, off, n, row in chunks:
            vr = vc_ref if is_cache else v_ref
            a = (e_refs[0][row:row + n, :] - ratio * e_refs[1][row:row + n, :]).astype(_BF)
            oc = _dot(vr[cols, off:off + n], a)
            acc = oc if acc is None else acc + oc
            yield
        o = (acc / l0).T
        o_ref[qt * tq:(qt + 1) * tq, cols] = (_rms(o, g_ref[...]) * (1.0 - lam_init)).astype(_BF)

    _interleave(scores(0))
    for u in range(len(units)):
        if u + 1 < len(units):
            _interleave(scores(u + 1), exponentials(u))
        else:
            _interleave(exponentials(u))
        _interleave(combine_pv(u))


def _diff_call(qk, vt, kc, vct, lam_qk, g_sub, batch, seq, lam_init):
    t = qk.shape[0]
    heads = DIFF_HEADS
    hw = 2 * DIFF_DH
    hb, nqt, tq = _attn_units(seq, heads, DIFF_Q_TILE, DIFF_UNITS)
    rows = nqt * tq
    nq = seq // rows
    nh = heads // hb
    past = 0 if kc is None else kc.shape[0] // batch
    in_specs = [pl.BlockSpec((rows, hb * hw), lambda b, h, i: (b * nq + i, h)),
                pl.BlockSpec((seq, hb * hw), lambda b, h, i: (b, nh + h)),
                pl.BlockSpec((hb * hw, seq), lambda b, h, i: (h, b))]
    args = [qk, qk, vt]
    if past:
        in_specs += [pl.BlockSpec((past, hb * hw), lambda b, h, i: (b, h)),
                     pl.BlockSpec((hb * hw, past), lambda b, h, i: (h, b))]
        args += [kc, vct]
    in_specs += [pl.BlockSpec(lam_qk.shape, lambda b, h, i: (0, 0)),
                 pl.BlockSpec((1, hw), lambda b, h, i: (0, 0))]
    args += [lam_qk, g_sub]
    return pl.pallas_call(
        functools.partial(_diff_kernel, lam_init=lam_init, hb=hb, nqt=nqt, tq=tq, seq=seq, past=past),
        grid=(batch, nh, nq),
        in_specs=in_specs,
        out_specs=pl.BlockSpec((rows, hb * hw), lambda b, h, i: (b * nq + i, h)),
        out_shape=jax.ShapeDtypeStruct((t, heads * hw), _BF),
        scratch_shapes=[pltpu.VMEM((seq + past, tq), _F32)] * 6,
        compiler_params=_params("parallel", "parallel", "parallel"),
        name="diff_attn",
    )(*args)


POOL_PAD = 16


def _pool_kernel(pz_ref, w_ref, sc_ref, o_ref, pad_ref, *, seq):
    gc = POOL_GC
    t = lax.broadcasted_iota(jnp.int32, (seq, gc), 0)
    zeros = jnp.zeros((POOL_PAD, gc), _F32)
    pad_ref[0:POOL_PAD, :] = zeros
    pad_ref[POOL_PAD + seq:2 * POOL_PAD + seq, :] = zeros
    for g, w in enumerate(POOL_WINDOWS):
        x = pz_ref[:, g * gc:(g + 1) * gc]
        pad_ref[POOL_PAD:POOL_PAD + seq, :] = x
        lo, hi = w // 2, w - w // 2
        tot = pad_ref[POOL_PAD - lo:POOL_PAD - lo + seq, :]
        for dlt in range(-lo + 1, hi):
            tot = tot + pad_ref[POOL_PAD + dlt:POOL_PAD + dlt + seq, :]
        cnt = (jnp.minimum(t + hi, seq) - jnp.maximum(t - lo, 0)).astype(_F32)
        pooled = (tot / cnt - x).astype(_BF)
        y = _dot(pooled, w_ref[g]) * sc_ref[:, g * gc:(g + 1) * gc]
        o_ref[:, g * gc:(g + 1) * gc] = y.astype(_BF)


def _pool_call(pz, w_pool, pool_scale, batch, seq):
    t, pw = pz.shape
    assert max(POOL_WINDOWS) <= POOL_PAD
    return pl.pallas_call(
        functools.partial(_pool_kernel, seq=seq),
        grid=(batch,),
        in_specs=[pl.BlockSpec((seq, pw), lambda b: (b, 0)),
                  pl.BlockSpec(w_pool.shape, lambda b: (0, 0, 0)),
                  pl.BlockSpec((1, pw), lambda b: (0, 0))],
        out_specs=pl.BlockSpec((seq, pw), lambda b: (b, 0)),
        out_shape=jax.ShapeDtypeStruct((t, pw), _BF),
        scratch_shapes=[pltpu.VMEM((seq + 2 * POOL_PAD, POOL_GC), _F32)],
        compiler_params=_params("parallel"),
        name="pool",
    )(pz, w_pool, pool_scale)


def _rope_tables(seq, reps, pad):
    pos = jnp.arange(seq)
    half = ROT_SPAN
    inv = ROPE_BASE ** (-jnp.arange(half, dtype=_F32) / half)
    ang_r = (pos // GRID_W).astype(_F32)[:, None] * inv
    ang_c = (pos % GRID_W).astype(_F32)[:, None] * inv
    cos = jnp.concatenate([jnp.cos(ang_r)] * 2 + [jnp.cos(ang_c)] * 2, axis=-1)
    sin_r, sin_c = jnp.sin(ang_r), jnp.sin(ang_c)
    z = jnp.zeros_like(sin_r)
    sa = jnp.concatenate([-sin_r, z, -sin_c, z], axis=-1)
    sb = jnp.concatenate([z, sin_r, z, sin_c], axis=-1)
    cos, sa, sb = (jnp.tile(a, (1, reps)) for a in (cos, sa, sb))
    if pad:
        cos = jnp.concatenate([cos, jnp.ones((seq, pad), _F32)], axis=-1)
        sa = jnp.concatenate([sa, jnp.zeros((seq, pad), _F32)], axis=-1)
        sb = jnp.concatenate([sb, jnp.zeros((seq, pad), _F32)], axis=-1)
    return cos, sa, sb


def _prep_weights(w_in_a, w_q_up, w_kv_up, w_in_d):
    na, d, _ = w_in_a.shape
    ql, kl, fw = Q_LORA, KV_LORA, FNET_GROUPS * FNET_GC
    q, ckv, kr, fz = (w_in_a[..., :ql], w_in_a[..., ql:ql + kl], w_in_a[..., ql + kl:ql + kl + MLA_ROPE],
                      w_in_a[..., ql + kl + MLA_ROPE:])
    w_in = jnp.concatenate([q, ckv, fz, kr, jnp.zeros((na, d, LANES - MLA_ROPE), _F32)], axis=-1).astype(_BF)
    scale = LOG2_E / math.sqrt(MLA_NOPE + MLA_ROPE)
    wq = (w_q_up * scale).reshape(na, ql, MLA_HEADS, MLA_NOPE + MLA_ROPE)
    wq = jnp.pad(wq, ((0, 0), (0, 0), (0, 0), (0, LANES - MLA_ROPE))).reshape(na, ql, -1).astype(_BF)
    wkv = w_kv_up.astype(_BF)
    pw, qk = len(POOL_WINDOWS) * POOL_GC, DIFF_HEADS * 2 * DIFF_DH
    wd = jnp.concatenate([w_in_d[..., :pw], w_in_d[..., pw:pw + qk] * (LOG2_E / math.sqrt(DIFF_DH)),
                          w_in_d[..., pw + qk:]], axis=-1).astype(_BF)
    return w_in, wq, wkv, wd


def _trunk(x, mod, batch, seq, caches, p, tpb_of):
    t, d = x.shape
    depth = p["g_norm"].shape[0]
    decode = caches is not None
    tm = min(TOKEN_TILE, seq) if decode else min(TOKEN_TILE, t)
    tpb = seq // tm if decode else None
    states = ([], [], [], [])
    if decode:
        tab_mla = _rope_tables(seq, 1, LANES - MLA_ROPE)
        tab_diff = _rope_tables(seq, LANES // DIFF_DH, 0)
    else:
        tab_mla = tab_diff = None
    dft_s = _dft_tables(seq)
    dft_c = _dft_tables(FNET_GC)
    aw = MLA_HEADS * MLA_V
    pw = len(POOL_WINDOWS) * POOL_GC
    qk = DIFF_HEADS * 2 * DIFF_DH
    for l in range(depth):
        i = l // 2
        g = p["g_norm"][l]
        x = _ffn_call(x, mod, g[0:1], p["wg"], p["wu"], p["wd"], l, 0, tm, tpb)
        if l % 2 == 0:
            outs = _proj_a_call(x, mod, g[1:2], p["w_in_a"][i], p["g_q"][i:i + 1], p["g_kv"][i:i + 1], p["wq"][i],
                                tab_mla, l, tm, tpb, not decode)
            q, ckv, kr, fz = outs[:4]
            if not decode:
                states[0].append(outs[4])
                states[1].append(outs[5][:, :MLA_ROPE])
            k, v = _kvup_call(ckv, kr, p["wkv"][i], tm)
            kc = vc = None
            if decode:
                past = caches[0].shape[2]
                c_ckv = caches[0][:, i].reshape(batch * past, KV_LORA).astype(_BF)
                c_kr = jnp.pad(caches[1][:, i].reshape(batch * past, MLA_ROPE), ((0, 0), (0, LANES - MLA_ROPE))).astype(_BF)
                kc, vc = _kvup_call(c_ckv, c_kr, p["wkv"][i], min(TOKEN_TILE, past))
            attn = _mla_call(q, k, v, kc, vc, batch, seq)
            four = _fourier_call(fz, dft_s, dft_c, p["w_fnet"][i], batch, seq)
            x = _oproj_call(attn, four, p["w_o_a"][i, :aw], p["w_o_a"][i, aw:], x, mod, l, tm, tpb)
        else:
            lam_init = 0.8 - 0.6 * math.exp(-0.3 * l)
            tm_d = min(PROJ_D_TILE, seq) if decode else tm
            outs = _proj_d_call(x, mod, g[1:2], p["w_in_d"][i], tab_diff, l, tm_d, seq // tm_d if decode else None,
                                not decode)
            pz, qk_new, vt_new = outs[:3]
            if not decode:
                states[2].append(outs[3][:, :qk])
                states[3].append(outs[3][:, qk:])
            kc = vct = None
            if decode:
                past = caches[2].shape[2]
                kc = caches[2][:, i].reshape(batch * past, qk).astype(_BF)
                vct = caches[3][:, i].reshape(batch * past, qk).T.astype(_BF)
            o = _diff_call(qk_new, vt_new, kc, vct, p["lam_qk"][i], p["g_sub"][i:i + 1], batch, seq, lam_init)
            pool = _pool_call(pz, p["w_pool"][i], p["pool_scale"][i:i + 1], batch, seq)
            x = _oproj_call(pool, o, p["w_o_d"][i, :pw], p["w_o_d"][i, pw:], x, mod, l, tm, tpb)
        x = _ffn_call(x, mod, g[2:3], p["wg"], p["wu"], p["wd"], l, 1, tm, tpb,
                      g_final=p["g_final"] if l == depth - 1 else None)
    return x, states


def kernel(x_prompt, x_sample, cache_mla_ckv, cache_mla_krope, cache_diff_k, cache_diff_v, c, c_ctx, w_mod, b_mod, g_norm, w_ffn_gate, w_ffn_up, w_ffn_down, w_in_a, g_q, g_kv, w_q_up, w_kv_up, w_fnet, w_o_a, w_in_d, lam_qk, g_sub, w_pool, pool_scale, w_o_d, g_final):
    batch, seq, d = x_prompt.shape
    dbatch, dseq, _ = x_sample.shape
    assert 1 + dbatch <= MOD_ROWS
    cond = jnp.concatenate([c_ctx[None, :], c, jnp.zeros((MOD_ROWS - 1 - dbatch, d), _F32)], axis=0)
    mod = _mod_call(cond, w_mod, b_mod)

    w_in, wq, wkv, wd_in = _prep_weights(w_in_a, w_q_up, w_kv_up, w_in_d)
    p = dict(g_norm=g_norm, wg=w_ffn_gate.astype(_BF), wu=w_ffn_up.astype(_BF), wd=w_ffn_down.astype(_BF),
             w_in_a=w_in, g_q=g_q, g_kv=g_kv, wq=wq, wkv=wkv, w_fnet=w_fnet.astype(_BF), w_o_a=w_o_a.astype(_BF),
             w_in_d=wd_in, lam_qk=lam_qk, g_sub=g_sub, w_pool=w_pool.astype(_BF), pool_scale=pool_scale,
             w_o_d=w_o_d.astype(_BF), g_final=g_final[None, :])

    y_p, st = _trunk(x_prompt.reshape(batch * seq, d), mod, batch, seq, None, p, None)
    caches = (cache_mla_ckv, cache_mla_krope, cache_diff_k, cache_diff_v)
    y_s, _ = _trunk(x_sample.reshape(dbatch * dseq, d), mod, dbatch, dseq, caches, p, None)

    new_ckv = jnp.stack([s.reshape(batch, seq, KV_LORA) for s in st[0]], axis=1)
    new_kr = jnp.stack([s.reshape(batch, seq, MLA_ROPE) for s in st[1]], axis=1)
    new_k = jnp.stack([s.reshape(batch, seq, DIFF_HEADS, 2, DIFF_DH) for s in st[2]], axis=1)
    new_v = jnp.stack([s.reshape(batch, seq, DIFF_HEADS, 2 * DIFF_DH) for s in st[3]], axis=1)
    return (y_p.reshape(batch, seq, d), y_s.reshape(dbatch, dseq, d), new_ckv, new_kr, new_k, new_v)
```

```python
import functools
import math

import jax
import jax.numpy as jnp
from jax import lax
from jax.experimental import pallas as pl
from jax.experimental.pallas import tpu as pltpu

GRID_W = 64
MLA_HEADS = 12
MLA_NOPE = 128
MLA_ROPE = 64
MLA_V = 128
Q_LORA = 512
KV_LORA = 512
FNET_GROUPS = 4
FNET_GC = 128
POOL_WINDOWS = (2, 4, 8, 16)
POOL_GC = 128
DIFF_HEADS = 12
DIFF_DH = 64
N_MOD = 9
ROPE_BASE = 10000.0
EPS = 1e-6
LOG2_E = math.log2(math.e)
ROT_SPAN = MLA_ROPE // 4
assert DIFF_DH == MLA_ROPE

LANES = 128
VMEM_BYTES = 64 * 2 ** 20
VMEM_LIMIT = VMEM_BYTES - 8 * 2 ** 20

TOKEN_TILE = 512
FFN_TILE = 512
FFN_ROW_TILE = 1024
PROJ_D_ROW_CHUNK = 256
PROJ_D_TILE = 1024
MLA_Q_TILE = 256
MLA_UNITS = 8
DIFF_Q_TILE = 256
DIFF_UNITS = 8
ATTN_KEY_CHUNK = 512
MOD_ROWS = 16

_BF = jnp.bfloat16
_F32 = jnp.float32


def _tile(n, pref):
    if n <= pref:
        return n
    t = (pref // LANES) * LANES
    while n % t:
        t -= LANES
    return t


def _params(*sem):
    return pltpu.CompilerParams(dimension_semantics=sem, vmem_limit_bytes=VMEM_LIMIT)


def _dot(a, b):
    return jnp.dot(a, b, preferred_element_type=_F32)


def _dot_t(a, b):
    return lax.dot_general(a, b, (((1,), (1,)), ((), ())), preferred_element_type=_F32)


def _rms(x, g):
    ms = jnp.mean(x * x, axis=-1, keepdims=True)
    return x * lax.rsqrt(ms + EPS) * g


def _norm_mod(x, g, shift, scale):
    w = g * (1.0 + scale)
    ms = jnp.mean(x * x, axis=-1, keepdims=True)
    return (x * lax.rsqrt(ms + EPS)) * w + shift


def _silu(x):
    return x / (1.0 + jnp.exp(-x))


def _rope(x, c, sa, sb):
    return x * c + pltpu.roll(x, LANES - ROT_SPAN, 1) * sa + pltpu.roll(x, ROT_SPAN, 1) * sb


def _mod_row(layer, tiles_per_batch):
    if tiles_per_batch is None:
        return lambda i: layer * MOD_ROWS
    return lambda i: layer * MOD_ROWS + 1 + i // tiles_per_batch


def _mod_kernel(c_ref, w_ref, b_ref, o_ref):
    s = _silu(c_ref[...]).astype(_BF)
    o_ref[...] = _dot(s, w_ref[...].astype(_BF)) + b_ref[...]


def _mod_call(cond, w_mod, b_mod):
    depth, d, n = w_mod.shape
    r = cond.shape[0]
    tn = _tile(n, 1024)
    out = pl.pallas_call(
        _mod_kernel,
        grid=(depth, n // tn),
        in_specs=[pl.BlockSpec((r, d), lambda l, j: (0, 0)),
                  pl.BlockSpec((None, d, tn), lambda l, j: (l, 0, j)),
                  pl.BlockSpec((None, 1, tn), lambda l, j: (l, 0, j))],
        out_specs=pl.BlockSpec((None, r, tn), lambda l, j: (l, 0, j)),
        out_shape=jax.ShapeDtypeStruct((depth, r, n), _F32),
        compiler_params=_params("parallel", "parallel"),
        name="mod",
    )(cond, w_mod, b_mod.reshape(depth, 1, n))
    return out.reshape(depth * r, N_MOD, d)


def _ffn_kernel(x_ref, m_ref, g_ref, wg_ref, wu_ref, wd_ref, gf_ref, o_ref, h_ref, acc_ref, *, base, final):
    j = pl.program_id(1)

    @pl.when(j == 0)
    def _():
        h = _norm_mod(x_ref[...], g_ref[...], m_ref[base:base + 1, :], m_ref[base + 1:base + 2, :])
        h_ref[...] = h.astype(_BF)
        acc_ref[...] = jnp.zeros_like(acc_ref)

    h = h_ref[...]
    a = _silu(_dot(h, wg_ref[...])) * _dot(h, wu_ref[...])
    acc_ref[...] += _dot(a.astype(_BF), wd_ref[...])

    @pl.when(j == pl.num_programs(1) - 1)
    def _():
        y = x_ref[...] + (0.5 * m_ref[base + 2:base + 3, :]) * acc_ref[...]
        if final:
            y = _rms(y, gf_ref[...])
        o_ref[...] = y


def _ffn_call(x, mod, g, wg, wu, wd, layer, sub, tm, tpb, g_final=None):
    t, d = x.shape
    f = wg.shape[-1]
    tf = _tile(f, FFN_TILE)
    row = _mod_row(layer, tpb)
    final = g_final is not None
    gf = g_final if final else g
    return pl.pallas_call(
        functools.partial(_ffn_kernel, base=0 if sub == 0 else 6, final=final),
        grid=(t // tm, f // tf),
        in_specs=[pl.BlockSpec((tm, d), lambda i, j: (i, 0), pipeline_mode=pl.Buffered(1)),
                  pl.BlockSpec((None, N_MOD, d), lambda i, j: (row(i), 0, 0)),
                  pl.BlockSpec((1, d), lambda i, j: (0, 0)),
                  pl.BlockSpec((None, None, d, tf), lambda i, j: (layer, sub, 0, j)),
                  pl.BlockSpec((None, None, d, tf), lambda i, j: (layer, sub, 0, j)),
                  pl.BlockSpec((None, None, tf, d), lambda i, j: (layer, sub, j, 0)),
                  pl.BlockSpec((1, d), lambda i, j: (0, 0))],
        out_specs=pl.BlockSpec((tm, d), lambda i, j: (i, 0), pipeline_mode=pl.Buffered(1)),
        out_shape=jax.ShapeDtypeStruct((t, d), _F32),
        scratch_shapes=[pltpu.VMEM((tm, d), _BF), pltpu.VMEM((tm, d), _F32)],
        compiler_params=_params("parallel", "arbitrary"),
        name="ffn",
    )(x, mod, g, wg, wu, wd, gf)


def _oproj_kernel(a1_ref, a2_ref, w1_ref, w2_ref, x_ref, m_ref, o_ref):
    acc = _dot(a1_ref[...], w1_ref[...]) + _dot(a2_ref[...], w2_ref[...])
    o_ref[...] = x_ref[...] + m_ref[5:6, :] * acc


def _oproj_call(a1, a2, w1, w2, x, mod, layer, tm, tpb):
    t, d = x.shape
    k1, k2 = a1.shape[1], a2.shape[1]
    row = _mod_row(layer, tpb)
    return pl.pallas_call(
        _oproj_kernel,
        grid=(t // tm,),
        in_specs=[pl.BlockSpec((tm, k1), lambda i: (i, 0)),
                  pl.BlockSpec((tm, k2), lambda i: (i, 0)),
                  pl.BlockSpec((k1, d), lambda i: (0, 0)),
                  pl.BlockSpec((k2, d), lambda i: (0, 0)),
                  pl.BlockSpec((tm, d), lambda i: (i, 0)),
                  pl.BlockSpec((None, N_MOD, d), lambda i: (row(i), 0, 0))],
        out_specs=pl.BlockSpec((tm, d), lambda i: (i, 0)),
        out_shape=jax.ShapeDtypeStruct((t, d), _F32),
        compiler_params=_params("parallel"),
        name="oproj",
    )(a1, a2, w1, w2, x, mod)


def _proj_a_kernel(*refs, rope, state, heads):
    x_ref, m_ref, g_ref, win_ref, gq_ref, gkv_ref, wq_ref = refs[:7]
    refs = refs[7:]
    if rope:
        c_ref, sa_ref, sb_ref = refs[:3]
        refs = refs[3:]
    q_ref, ckv_ref, kr_ref, fz_ref = refs[:4]
    h = _norm_mod(x_ref[...], g_ref[...], m_ref[3:4, :], m_ref[4:5, :]).astype(_BF)
    u = _dot(h, win_ref[...])
    ql, kl = Q_LORA, KV_LORA
    fw = FNET_GROUPS * FNET_GC
    qn = _rms(u[:, :ql], gq_ref[...]).astype(_BF)
    ckv = _rms(u[:, ql:ql + kl], gkv_ref[...])
    kr = u[:, ql + kl + fw:]
    ckv_ref[...] = ckv.astype(_BF)
    fz_ref[...] = u[:, ql + kl:ql + kl + fw].astype(_BF)
    if state:
        refs[4][...] = ckv
        refs[5][...] = kr
    q = _dot(qn, wq_ref[...])
    hw = MLA_NOPE + LANES
    if rope:
        c, sa, sb = c_ref[...], sa_ref[...], sb_ref[...]
        kr = _rope(kr, c, sa, sb)
        for hd in range(heads):
            q_ref[:, hd * hw:hd * hw + MLA_NOPE] = q[:, hd * hw:hd * hw + MLA_NOPE].astype(_BF)
            q_ref[:, hd * hw + MLA_NOPE:(hd + 1) * hw] = _rope(q[:, hd * hw + MLA_NOPE:(hd + 1) * hw], c, sa, sb).astype(_BF)
    else:
        q_ref[...] = q.astype(_BF)
    kr_ref[...] = kr.astype(_BF)


def _proj_a_call(x, mod, g, w_in, g_q, g_kv, wq, tables, layer, tm, tpb, state):
    t, d = x.shape
    n_in = w_in.shape[1]
    nq = wq.shape[1]
    fw = FNET_GROUPS * FNET_GC
    rope = tables is not None
    row = _mod_row(layer, tpb)
    in_specs = [pl.BlockSpec((tm, d), lambda i: (i, 0)),
                pl.BlockSpec((None, N_MOD, d), lambda i: (row(i), 0, 0)),
                pl.BlockSpec((1, d), lambda i: (0, 0)),
                pl.BlockSpec((d, n_in), lambda i: (0, 0)),
                pl.BlockSpec((1, Q_LORA), lambda i: (0, 0)),
                pl.BlockSpec((1, KV_LORA), lambda i: (0, 0)),
                pl.BlockSpec((Q_LORA, nq), lambda i: (0, 0))]
    args = [x, mod, g, w_in, g_q, g_kv, wq]
    if rope:
        in_specs += [pl.BlockSpec((tm, LANES), lambda i: (i % tpb, 0))] * 3
        args += list(tables)
    shapes = [((t, nq), _BF), ((t, KV_LORA), _BF), ((t, LANES), _BF), ((t, fw), _BF)]
    if state:
        shapes += [((t, KV_LORA), _F32), ((t, LANES), _F32)]
    return pl.pallas_call(
        functools.partial(_proj_a_kernel, rope=rope, state=state, heads=MLA_HEADS),
        grid=(t // tm,),
        in_specs=in_specs,
        out_specs=[pl.BlockSpec((tm, s[1]), lambda i: (i, 0)) for s, _ in shapes],
        out_shape=[jax.ShapeDtypeStruct(s, dt) for s, dt in shapes],
        compiler_params=_params("parallel"),
        name="proj_a",
    )(*args)


def _kvup_kernel(ckv_ref, kr_ref, w_ref, k_ref, v_ref, *, heads):
    kv = _dot(ckv_ref[...], w_ref[...])
    kr = kr_ref[...]
    hw = MLA_NOPE + MLA_V
    for hd in range(heads):
        k_ref[hd, :, :MLA_NOPE] = kv[:, hd * hw:hd * hw + MLA_NOPE].astype(_BF)
        k_ref[hd, :, MLA_NOPE:] = kr
        v_ref[hd] = kv[:, hd * hw + MLA_NOPE:(hd + 1) * hw].T.astype(_BF)


def _kvup_call(ckv, kr, wkv, tm):
    t = ckv.shape[0]
    heads = MLA_HEADS
    return pl.pallas_call(
        functools.partial(_kvup_kernel, heads=heads),
        grid=(t // tm,),
        in_specs=[pl.BlockSpec((tm, KV_LORA), lambda i: (i, 0)),
                  pl.BlockSpec((tm, LANES), lambda i: (i, 0)),
                  pl.BlockSpec(wkv.shape, lambda i: (0, 0))],
        out_specs=[pl.BlockSpec((heads, tm, MLA_NOPE + LANES), lambda i: (0, i, 0)),
                   pl.BlockSpec((heads, MLA_V, tm), lambda i: (0, 0, i))],
        out_shape=[jax.ShapeDtypeStruct((heads, t, MLA_NOPE + LANES), _BF),
                   jax.ShapeDtypeStruct((heads, MLA_V, t), _BF)],
        compiler_params=_params("parallel"),
        name="kvup",
    )(ckv, kr, wkv)


def _interleave(*stages):
    live = list(stages)
    while live:
        for st in list(live):
            try:
                next(st)
            except StopIteration:
                live.remove(st)


def _key_chunks(seq, past):
    kc = min(ATTN_KEY_CHUNK, seq)
    chunks = [(False, c * kc, kc, c * kc) for c in range(seq // kc)]
    if past:
        pc = min(ATTN_KEY_CHUNK, past)
        chunks += [(True, c * pc, pc, seq + c * pc) for c in range(past // pc)]
    return chunks


def _attn_units(seq, heads, q_tile, n_units):
    tq = min(q_tile, seq)
    nqt = min(n_units, seq // tq)
    hb = math.gcd(heads, max(n_units // nqt, 1))
    return hb, nqt, tq


def _mla_kernel(*refs, hb, nqt, tq, seq, past):
    if past:
        q_ref, k_ref, v_ref, kc_ref, vc_ref, o_ref, *s_refs = refs
    else:
        q_ref, k_ref, v_ref, o_ref, *s_refs = refs
        kc_ref = vc_ref = None
    qw = MLA_NOPE + LANES
    chunks = _key_chunks(seq, past)
    units = [(hd, qt) for hd in range(hb) for qt in range(nqt)]
    col_max = [None] * len(units)

    def scores(u):
        hd, qt = units[u]
        q = q_ref[qt * tq:(qt + 1) * tq, hd * qw:(hd + 1) * qw]
        m = None
        for is_cache, off, n, row in chunks:
            kr = kc_ref if is_cache else k_ref
            s = _dot_t(kr[hd, off:off + n, :], q)
            s_refs[u % 2][row:row + n, :] = s
            mc = jnp.max(s, axis=0, keepdims=True)
            m = mc if m is None else jnp.maximum(m, mc)
            yield
        col_max[u] = m

    def softmax_pv(u):
        hd, qt = units[u]
        m = col_max[u]
        l = acc = None
        for is_cache, off, n, row in chunks:
            vr = vc_ref if is_cache else v_ref
            p = jnp.exp2(s_refs[u % 2][row:row + n, :] - m)
            lc = jnp.sum(p, axis=0, keepdims=True)
            oc = _dot(vr[hd, :, off:off + n], p.astype(_BF))
            l = lc if l is None else l + lc
            acc = oc if acc is None else acc + oc
            yield
        o_ref[qt * tq:(qt + 1) * tq, hd * MLA_V:(hd + 1) * MLA_V] = (acc / l).T.astype(_BF)

    _interleave(scores(0))
    for u in range(len(units)):
        if u + 1 < len(units):
            _interleave(scores(u + 1), softmax_pv(u))
        else:
            _interleave(softmax_pv(u))


def _mla_call(q, k, v, kc, vc, batch, seq):
    t = q.shape[0]
    heads = MLA_HEADS
    hb, nqt, tq = _attn_units(seq, heads, MLA_Q_TILE, MLA_UNITS)
    rows = nqt * tq
    nq = seq // rows
    qw = MLA_NOPE + LANES
    past = 0 if kc is None else kc.shape[1] // batch
    in_specs = [pl.BlockSpec((rows, hb * qw), lambda b, h, i: (b * nq + i, h)),
                pl.BlockSpec((hb, seq, qw), lambda b, h, i: (h, b, 0)),
                pl.BlockSpec((hb, MLA_V, seq), lambda b, h, i: (h, 0, b))]
    args = [q, k, v]
    if past:
        in_specs += [pl.BlockSpec((hb, past, qw), lambda b, h, i: (h, b, 0)),
                     pl.BlockSpec((hb, MLA_V, past), lambda b, h, i: (h, 0, b))]
        args += [kc, vc]
    return pl.pallas_call(
        functools.partial(_mla_kernel, hb=hb, nqt=nqt, tq=tq, seq=seq, past=past),
        grid=(batch, heads // hb, nq),
        in_specs=in_specs,
        out_specs=pl.BlockSpec((rows, hb * MLA_V), lambda b, h, i: (b * nq + i, h)),
        out_shape=jax.ShapeDtypeStruct((t, heads * MLA_V), _BF),
        scratch_shapes=[pltpu.VMEM((seq + past, tq), _F32)] * 2,
        compiler_params=_params("parallel", "parallel", "parallel"),
        name="mla_attn",
    )(*args)


def _fourier_kernel(cs_ref, sn_ref, z_ref, cc_ref, sc_ref, w_ref, o_ref, *, norm):
    z = z_ref[...]
    a = _dot(cs_ref[...], z)
    b = _dot(sn_ref[...], z)
    gc = FNET_GC
    for g in range(FNET_GROUPS):
        f = _dot(a[:, g * gc:(g + 1) * gc].astype(_BF), cc_ref[...]) - _dot(b[:, g * gc:(g + 1) * gc].astype(_BF), sc_ref[...])
        f = (f * norm).astype(_BF)
        o_ref[:, g * gc:(g + 1) * gc] = _dot(f, w_ref[g]).astype(_BF)


def _dft_tables(n):
    r1 = 1 << (int(math.log2(n)) // 2)
    assert n % r1 == 0
    r2 = n // r1

    def unit(rows, period):
        idx = lax.broadcasted_iota(jnp.int32, (rows, n), 0) * lax.broadcasted_iota(jnp.int32, (rows, n), 1) % period
        ang = idx.astype(_F32) * (2.0 * math.pi / period)
        return jnp.cos(ang), jnp.sin(ang)

    ca, sa = (t[:, None, :] for t in unit(r1, r1))
    cb, sb = (t[None, :, :] for t in unit(r2, n))
    cos = (ca * cb - sa * sb).reshape(n, n)
    sin = (sa * cb + ca * sb).reshape(n, n)
    return cos.astype(_BF), sin.astype(_BF)


def _fourier_call(z, tabs_s, tabs_c, w, batch, seq):
    t, fw = z.shape
    tq = min(TOKEN_TILE, seq)
    nq = seq // tq
    cs, sn = tabs_s
    cc, sc = tabs_c
    gc = FNET_GC
    return pl.pallas_call(
        functools.partial(_fourier_kernel, norm=1.0 / math.sqrt(seq * gc)),
        grid=(nq, batch),
        in_specs=[pl.BlockSpec((tq, seq), lambda i, b: (i, 0)),
                  pl.BlockSpec((tq, seq), lambda i, b: (i, 0)),
                  pl.BlockSpec((seq, fw), lambda i, b: (b, 0)),
                  pl.BlockSpec((gc, gc), lambda i, b: (0, 0)),
                  pl.BlockSpec((gc, gc), lambda i, b: (0, 0)),
                  pl.BlockSpec(w.shape, lambda i, b: (0, 0, 0))],
        out_specs=pl.BlockSpec((tq, fw), lambda i, b: (b * nq + i, 0)),
        out_shape=jax.ShapeDtypeStruct((t, fw), _BF),
        compiler_params=_params("parallel", "parallel"),
        name="fourier",
    )(cs, sn, z, cc, sc, w)


def _proj_d_kernel(*refs, rope, state, npz, nqk, chunks):
    x_ref, m_ref, g_ref, w_ref = refs[:4]
    refs = refs[4:]
    if rope:
        c_ref, sa_ref, sb_ref = refs[:3]
        refs = refs[3:]
    pz_ref, qk_ref, vt_ref = refs[:3]
    h_ref = refs[-1]
    j = pl.program_id(1)

    @pl.when(j == 0)
    def _():
        h_ref[...] = _norm_mod(x_ref[...], g_ref[...], m_ref[3:4, :], m_ref[4:5, :]).astype(_BF)

    tm = h_ref.shape[0]
    rc = min(PROJ_D_ROW_CHUNK, tm)
    row_chunks = [slice(r, r + rc) for r in range(0, tm, rc)]

    @pl.when(j < npz)
    def _():
        for rows in row_chunks:
            pz_ref[rows, :] = _dot(h_ref[rows, :], w_ref[...])

    def rotary_tiles(keep):
        for rows in row_chunks:
            acc = _dot(h_ref[rows, :], w_ref[...])
            if keep:
                refs[3][rows, :] = acc
            if rope:
                c, sa, sb = c_ref[rows, :], sa_ref[rows, :], sb_ref[rows, :]
                for k in range(chunks):
                    cols = slice(k * LANES, (k + 1) * LANES)
                    qk_ref[rows, cols] = _rope(acc[:, cols], c, sa, sb).astype(_BF)
            else:
                qk_ref[rows, :] = acc.astype(_BF)

    @pl.when((j >= npz) & (j < npz + nqk))
    def _():
        rotary_tiles(False)

    @pl.when((j >= npz + nqk) & (j < npz + 2 * nqk))
    def _():
        rotary_tiles(state)

    @pl.when(j >= npz + 2 * nqk)
    def _():
        for rows in row_chunks:
            acc = _dot(h_ref[rows, :], w_ref[...])
            if state:
                refs[3][rows, :] = acc
            vt_ref[:, rows] = acc.T.astype(_BF)


def _proj_d_call(x, mod, g, w_in, tables, layer, tm, tpb, state):
    t, d = x.shape
    n = w_in.shape[1]
    pw = len(POOL_WINDOWS) * POOL_GC
    qk = DIFF_HEADS * 2 * DIFF_DH
    tn = math.gcd(math.gcd(pw, qk), 512)
    npz, nqk = pw // tn, qk // tn
    rope = tables is not None
    row = _mod_row(layer, tpb)
    in_specs = [pl.BlockSpec((tm, d), lambda i, j: (i, 0)),
                pl.BlockSpec((None, N_MOD, d), lambda i, j: (row(i), 0, 0)),
                pl.BlockSpec((1, d), lambda i, j: (0, 0)),
                pl.BlockSpec((d, tn), lambda i, j: (0, j))]
    args = [x, mod, g, w_in]
    if rope:
        in_specs += [pl.BlockSpec((tm, LANES), lambda i, j: (i % tpb, 0))] * 3
        args += list(tables)
    out_specs = [pl.BlockSpec((tm, tn), lambda i, j: (i, jnp.minimum(j, npz - 1))),
                 pl.BlockSpec((tm, tn), lambda i, j: (i, jnp.clip(j - npz, 0, 2 * nqk - 1))),
                 pl.BlockSpec((tn, tm), lambda i, j: (jnp.maximum(j - npz - 2 * nqk, 0), i))]
    out_shape = [jax.ShapeDtypeStruct((t, pw), _F32), jax.ShapeDtypeStruct((t, 2 * qk), _BF),
                 jax.ShapeDtypeStruct((qk, t), _BF)]
    if state:
        out_specs.append(pl.BlockSpec((tm, tn), lambda i, j: (i, jnp.maximum(j - npz - nqk, 0))))
        out_shape.append(jax.ShapeDtypeStruct((t, 2 * qk), _F32))
    return pl.pallas_call(
        functools.partial(_proj_d_kernel, rope=rope, state=state, npz=npz, nqk=nqk, chunks=tn // LANES),
        grid=(t // tm, n // tn),
        in_specs=in_specs,
        out_specs=out_specs,
        out_shape=out_shape,
        scratch_shapes=[pltpu.VMEM((tm, d), _BF)],
        compiler_params=_params("parallel", "arbitrary"),
        name="proj_d",
    )(*args)


def _diff_kernel(*refs, lam_init, hb, nqt, tq, seq, past):
    if past:
        q_ref, k_ref, v_ref, kc_ref, vc_ref, lam_ref, g_ref, o_ref, *scr = refs
    else:
        q_ref, k_ref, v_ref, lam_ref, g_ref, o_ref, *scr = refs
        kc_ref = vc_ref = None
    s_refs, e_refs = (scr[0:2], scr[2:4]), scr[4:6]
    lq = lam_ref[...]
    lam = (jnp.exp(jnp.sum(lq[0:1] * lq[1:2], axis=-1, keepdims=True))
           - jnp.exp(jnp.sum(lq[2:3] * lq[3:4], axis=-1, keepdims=True)) + lam_init)
    hw = 2 * DIFF_DH
    lane = lax.broadcasted_iota(jnp.int32, (tq, hw), 1)
    chunks = _key_chunks(seq, past)
    units = [(hd, qt) for hd in range(hb) for qt in range(nqt)]
    col_max = [None] * len(units)
    col_sum = [None] * len(units)

    def scores(u):
        hd, qt = units[u]
        cols = slice(hd * hw, (hd + 1) * hw)
        q = q_ref[qt * tq:(qt + 1) * tq, cols]
        zero = jnp.zeros_like(q)
        halves = (jnp.where(lane < DIFF_DH, q, zero), jnp.where(lane >= DIFF_DH, q, zero))
        m = [None, None]
        for is_cache, off, n, row in chunks:
            kr = kc_ref if is_cache else k_ref
            k = kr[off:off + n, cols]
            for c in range(2):
                s = _dot_t(k, halves[c])
                s_refs[u % 2][c][row:row + n, :] = s
                mc = jnp.max(s, axis=0, keepdims=True)
                m[c] = mc if m[c] is None else jnp.maximum(m[c], mc)
            yield
        col_max[u] = m

    def exponentials(u):
        m = col_max[u]
        l = [None, None]
        for _, _, n, row in chunks:
            for c in range(2):
                e = jnp.exp2(s_refs[u % 2][c][row:row + n, :] - m[c])
                e_refs[c][row:row + n, :] = e
                lc = jnp.sum(e, axis=0, keepdims=True)
                l[c] = lc if l[c] is None else l[c] + lc
            yield
        col_sum[u] = l

    def combine_pv(u):
        hd, qt = units[u]
        cols = slice(hd * hw, (hd + 1) * hw)
        l0, l1 = col_sum[u]
        ratio = lam * l0 / l1
        acc = None
        for is_cache, off, n, row in chunks:
            vr = vc_ref if is_cache else v_ref
            a = (e_refs[0][row:row + n, :] - ratio * e_refs[1][row:row + n, :]).astype(_BF)
            oc = _dot(vr[cols, off:off + n], a)
            acc = oc if acc is None else acc + oc
            yield
        o = (acc / l0).T
        o_ref[qt * tq:(qt + 1) * tq, cols] = (_rms(o, g_ref[...]) * (1.0 - lam_init)).astype(_BF)

    _interleave(scores(0))
    for u in range(len(units)):
        if u + 1 < len(units):
            _interleave(scores(u + 1), exponentials(u))
        else:
            _interleave(exponentials(u))
        _interleave(combine_pv(u))


def _diff_call(qk, vt, kc, vct, lam_qk, g_sub, batch, seq, lam_init):
    t = qk.shape[0]
    heads = DIFF_HEADS
    hw = 2 * DIFF_DH
    hb, nqt, tq = _attn_units(seq, heads, DIFF_Q_TILE, DIFF_UNITS)
    rows = nqt * tq
    nq = seq // rows
    nh = heads // hb
    past = 0 if kc is None else kc.shape[0] // batch
    in_specs = [pl.BlockSpec((rows, hb * hw), lambda b, h, i: (b * nq + i, h)),
                pl.BlockSpec((seq, hb * hw), lambda b, h, i: (b, nh + h)),
                pl.BlockSpec((hb * hw, seq), lambda b, h, i: (h, b))]
    args = [qk, qk, vt]
    if past:
        in_specs += [pl.BlockSpec((past, hb * hw), lambda b, h, i: (b, h)),
                     pl.BlockSpec((hb * hw, past), lambda b, h, i: (h, b))]
        args += [kc, vct]
    in_specs += [pl.BlockSpec(lam_qk.shape, lambda b, h, i: (0, 0)),
                 pl.BlockSpec((1, hw), lambda b, h, i: (0, 0))]
    args += [lam_qk, g_sub]
    return pl.pallas_call(
        functools.partial(_diff_kernel, lam_init=lam_init, hb=hb, nqt=nqt, tq=tq, seq=seq, past=past),
        grid=(batch, nh, nq),
        in_specs=in_specs,
        out_specs=pl.BlockSpec((rows, hb * hw), lambda b, h, i: (b * nq + i, h)),
        out_shape=jax.ShapeDtypeStruct((t, heads * hw), _BF),
        scratch_shapes=[pltpu.VMEM((seq + past, tq), _F32)] * 6,
        compiler_params=_params("parallel", "parallel", "parallel"),
        name="diff_attn",
    )(*args)


POOL_PAD = 16


def _pool_kernel(pz_ref, w_ref, sc_ref, o_ref, pad_ref, *, seq):
    gc = POOL_GC
    t = lax.broadcasted_iota(jnp.int32, (seq, gc), 0)
    zeros = jnp.zeros((POOL_PAD, gc), _F32)
    pad_ref[0:POOL_PAD, :] = zeros
    pad_ref[POOL_PAD + seq:2 * POOL_PAD + seq, :] = zeros
    for g, w in enumerate(POOL_WINDOWS):
        x = pz_ref[:, g * gc:(g + 1) * gc]
        pad_ref[POOL_PAD:POOL_PAD + seq, :] = x
        lo, hi = w // 2, w - w // 2
        tot = pad_ref[POOL_PAD - lo:POOL_PAD - lo + seq, :]
        for dlt in range(-lo + 1, hi):
            tot = tot + pad_ref[POOL_PAD + dlt:POOL_PAD + dlt + seq, :]
        cnt = (jnp.minimum(t + hi, seq) - jnp.maximum(t - lo, 0)).astype(_F32)
        pooled = (tot / cnt - x).astype(_BF)
        y = _dot(pooled, w_ref[g]) * sc_ref[:, g * gc:(g + 1) * gc]
        o_ref[:, g * gc:(g + 1) * gc] = y.astype(_BF)


def _pool_call(pz, w_pool, pool_scale, batch, seq):
    t, pw = pz.shape
    assert max(POOL_WINDOWS) <= POOL_PAD
    return pl.pallas_call(
        functools.partial(_pool_kernel, seq=seq),
        grid=(batch,),
        in_specs=[pl.BlockSpec((seq, pw), lambda b: (b, 0)),
                  pl.BlockSpec(w_pool.shape, lambda b: (0, 0, 0)),
                  pl.BlockSpec((1, pw), lambda b: (0, 0))],
        out_specs=pl.BlockSpec((seq, pw), lambda b: (b, 0)),
        out_shape=jax.ShapeDtypeStruct((t, pw), _BF),
        scratch_shapes=[pltpu.VMEM((seq + 2 * POOL_PAD, POOL_GC), _F32)],
        compiler_params=_params("parallel"),
        name="pool",
    )(pz, w_pool, pool_scale)


def _rope_tables(seq, reps, pad):
    pos = jnp.arange(seq)
    half = ROT_SPAN
    inv = ROPE_BASE ** (-jnp.arange(half, dtype=_F32) / half)
    ang_r = (pos // GRID_W).astype(_F32)[:, None] * inv
    ang_c = (pos % GRID_W).astype(_F32)[:, None] * inv
    cos = jnp.concatenate([jnp.cos(ang_r)] * 2 + [jnp.cos(ang_c)] * 2, axis=-1)
    sin_r, sin_c = jnp.sin(ang_r), jnp.sin(ang_c)
    z = jnp.zeros_like(sin_r)
    sa = jnp.concatenate([-sin_r, z, -sin_c, z], axis=-1)
    sb = jnp.concatenate([z, sin_r, z, sin_c], axis=-1)
    cos, sa, sb = (jnp.tile(a, (1, reps)) for a in (cos, sa, sb))
    if pad:
        cos = jnp.concatenate([cos, jnp.ones((seq, pad), _F32)], axis=-1)
        sa = jnp.concatenate([sa, jnp.zeros((seq, pad), _F32)], axis=-1)
        sb = jnp.concatenate([sb, jnp.zeros((seq, pad), _F32)], axis=-1)
    return cos, sa, sb


def _prep_weights(w_in_a, w_q_up, w_kv_up, w_in_d):
    na, d, _ = w_in_a.shape
    ql, kl, fw = Q_LORA, KV_LORA, FNET_GROUPS * FNET_GC
    q, ckv, kr, fz = (w_in_a[..., :ql], w_in_a[..., ql:ql + kl], w_in_a[..., ql + kl:ql + kl + MLA_ROPE],
                      w_in_a[..., ql + kl + MLA_ROPE:])
    w_in = jnp.concatenate([q, ckv, fz, kr, jnp.zeros((na, d, LANES - MLA_ROPE), _F32)], axis=-1).astype(_BF)
    scale = LOG2_E / math.sqrt(MLA_NOPE + MLA_ROPE)
    wq = (w_q_up * scale).reshape(na, ql, MLA_HEADS, MLA_NOPE + MLA_ROPE)
    wq = jnp.pad(wq, ((0, 0), (0, 0), (0, 0), (0, LANES - MLA_ROPE))).reshape(na, ql, -1).astype(_BF)
    wkv = w_kv_up.astype(_BF)
    pw, qk = len(POOL_WINDOWS) * POOL_GC, DIFF_HEADS * 2 * DIFF_DH
    wd = jnp.concatenate([w_in_d[..., :pw], w_in_d[..., pw:pw + qk] * (LOG2_E / math.sqrt(DIFF_DH)),
                          w_in_d[..., pw + qk:]], axis=-1).astype(_BF)
    return w_in, wq, wkv, wd


def _trunk(x, mod, batch, seq, caches, p, tpb_of):
    t, d = x.shape
    depth = p["g_norm"].shape[0]
    decode = caches is not None
    tm = min(TOKEN_TILE, seq) if decode else min(TOKEN_TILE, t)
    tpb = seq // tm if decode else None
    tm_f = min(FFN_ROW_TILE, seq) if decode else min(FFN_ROW_TILE, t)
    tpb_f = seq // tm_f if decode else None
    states = ([], [], [], [])
    if decode:
        tab_mla = _rope_tables(seq, 1, LANES - MLA_ROPE)
        tab_diff = _rope_tables(seq, LANES // DIFF_DH, 0)
    else:
        tab_mla = tab_diff = None
    dft_s = _dft_tables(seq)
    dft_c = _dft_tables(FNET_GC)
    aw = MLA_HEADS * MLA_V
    pw = len(POOL_WINDOWS) * POOL_GC
    qk = DIFF_HEADS * 2 * DIFF_DH
    for l in range(depth):
        i = l // 2
        g = p["g_norm"][l]
        x = _ffn_call(x, mod, g[0:1], p["wg"], p["wu"], p["wd"], l, 0, tm_f, tpb_f)
        if l % 2 == 0:
            outs = _proj_a_call(x, mod, g[1:2], p["w_in_a"][i], p["g_q"][i:i + 1], p["g_kv"][i:i + 1], p["wq"][i],
                                tab_mla, l, tm, tpb, not decode)
            q, ckv, kr, fz = outs[:4]
            if not decode:
                states[0].append(outs[4])
                states[1].append(outs[5][:, :MLA_ROPE])
            k, v = _kvup_call(ckv, kr, p["wkv"][i], tm)
            kc = vc = None
            if decode:
                past = caches[0].shape[2]
                c_ckv = caches[0][:, i].reshape(batch * past, KV_LORA).astype(_BF)
                c_kr = jnp.pad(caches[1][:, i].reshape(batch * past, MLA_ROPE), ((0, 0), (0, LANES - MLA_ROPE))).astype(_BF)
                kc, vc = _kvup_call(c_ckv, c_kr, p["wkv"][i], min(TOKEN_TILE, past))
            attn = _mla_call(q, k, v, kc, vc, batch, seq)
            four = _fourier_call(fz, dft_s, dft_c, p["w_fnet"][i], batch, seq)
            x = _oproj_call(attn, four, p["w_o_a"][i, :aw], p["w_o_a"][i, aw:], x, mod, l, tm, tpb)
        else:
            lam_init = 0.8 - 0.6 * math.exp(-0.3 * l)
            tm_d = min(PROJ_D_TILE, seq) if decode else tm
            outs = _proj_d_call(x, mod, g[1:2], p["w_in_d"][i], tab_diff, l, tm_d, seq // tm_d if decode else None,
                                not decode)
            pz, qk_new, vt_new = outs[:3]
            if not decode:
                states[2].append(outs[3][:, :qk])
                states[3].append(outs[3][:, qk:])
            kc = vct = None
            if decode:
                past = caches[2].shape[2]
                kc = caches[2][:, i].reshape(batch * past, qk).astype(_BF)
                vct = caches[3][:, i].reshape(batch * past, qk).T.astype(_BF)
            o = _diff_call(qk_new, vt_new, kc, vct, p["lam_qk"][i], p["g_sub"][i:i + 1], batch, seq, lam_init)
            pool = _pool_call(pz, p["w_pool"][i], p["pool_scale"][i:i + 1], batch, seq)
            x = _oproj_call(pool, o, p["w_o_d"][i, :pw], p["w_o_d"][i, pw:], x, mod, l, tm, tpb)
        x = _ffn_call(x, mod, g[2:3], p["wg"], p["wu"], p["wd"], l, 1, tm_f, tpb_f,
                      g_final=p["g_final"] if l == depth - 1 else None)
    return x, states


def kernel(x_prompt, x_sample, cache_mla_ckv, cache_mla_krope, cache_diff_k, cache_diff_v, c, c_ctx, w_mod, b_mod, g_norm, w_ffn_gate, w_ffn_up, w_ffn_down, w_in_a, g_q, g_kv, w_q_up, w_kv_up, w_fnet, w_o_a, w_in_d, lam_qk, g_sub, w_pool, pool_scale, w_o_d, g_final):
    batch, seq, d = x_prompt.shape
    dbatch, dseq, _ = x_sample.shape
    assert 1 + dbatch <= MOD_ROWS
    cond = jnp.concatenate([c_ctx[None, :], c, jnp.zeros((MOD_ROWS - 1 - dbatch, d), _F32)], axis=0)
    mod = _mod_call(cond, w_mod, b_mod)

    w_in, wq, wkv, wd_in = _prep_weights(w_in_a, w_q_up, w_kv_up, w_in_d)
    p = dict(g_norm=g_norm, wg=w_ffn_gate.astype(_BF), wu=w_ffn_up.astype(_BF), wd=w_ffn_down.astype(_BF),
             w_in_a=w_in, g_q=g_q, g_kv=g_kv, wq=wq, wkv=wkv, w_fnet=w_fnet.astype(_BF), w_o_a=w_o_a.astype(_BF),
             w_in_d=wd_in, lam_qk=lam_qk, g_sub=g_sub, w_pool=w_pool.astype(_BF), pool_scale=pool_scale,
             w_o_d=w_o_d.astype(_BF), g_final=g_final[None, :])

    y_p, st = _trunk(x_prompt.reshape(batch * seq, d), mod, batch, seq, None, p, None)
    caches = (cache_mla_ckv, cache_mla_krope, cache_diff_k, cache_diff_v)
    y_s, _ = _trunk(x_sample.reshape(dbatch * dseq, d), mod, dbatch, dseq, caches, p, None)

    new_ckv = jnp.stack([s.reshape(batch, seq, KV_LORA) for s in st[0]], axis=1)
    new_kr = jnp.stack([s.reshape(batch, seq, MLA_ROPE) for s in st[1]], axis=1)
    new_k = jnp.stack([s.reshape(batch, seq, DIFF_HEADS, 2, DIFF_DH) for s in st[2]], axis=1)
    new_v = jnp.stack([s.reshape(batch, seq, DIFF_HEADS, 2 * DIFF_DH) for s in st[3]], axis=1)
    return (y_p.reshape(batch, seq, d), y_s.reshape(dbatch, dseq, d), new_ckv, new_kr, new_k, new_v)
```
